```python
import math
import jax, jax.numpy as jnp
from jax import lax
import numpy as np

D_MODEL = 1024
BATCH = 4
SEQ = 8192
DEPTH = 1

POOL_WIDTH = D_MODEL // 2
POOL_GROUPS = 4
POOL_WINDOWS = (2, 4, 8, 16)
POOL_GROUP_IN = POOL_WIDTH // POOL_GROUPS
POOL_GROUP_OUT = D_MODEL // POOL_GROUPS
CONV_WIDTH = D_MODEL // 2
CONV_KERNEL = 31
IN_COLS = POOL_WIDTH + 2 * CONV_WIDTH + 2 * D_MODEL
N_EXPERTS = 256
TOP_K = 8
N_GROUPS = 8
TOPK_GROUPS = 4
EXPERTS_PER_GROUP = N_EXPERTS // N_GROUPS
EXPERT_HIDDEN = D_MODEL // 4
SHARED_HIDDEN = EXPERT_HIDDEN
ROUTED_SCALE = 2.5
EXPERT_BLOCK = 128
LN_EPS = 1e-5
DEEPNORM_ALPHA = (2.0 * DEPTH) ** 0.25
DEEPNORM_BETA = (8.0 * DEPTH) ** -0.25

kernel_name = "hybrid_pool_conformer_moe_deepnorm"


def _layernorm(x, g, b):
    xf = x.astype(jnp.float32)
    mu = jnp.mean(xf, axis=-1, keepdims=True)
    var = jnp.mean(jnp.square(xf - mu), axis=-1, keepdims=True)
    out = (xf - mu) * lax.rsqrt(var + LN_EPS) * g.astype(jnp.float32) + b.astype(jnp.float32)
    return out.astype(x.dtype)


def _causal_pool_residual(u, window):
    uf = u.astype(jnp.float32)
    s = uf.shape[1]
    cs = jnp.cumsum(uf, axis=1)
    prev = jnp.pad(cs, ((0, 0), (window, 0), (0, 0)))[:, :s]
    count = jnp.minimum(jnp.arange(1, s + 1), window).astype(jnp.float32)
    mean = (cs - prev) / count[None, :, None]
    return (mean - uf).astype(u.dtype)


def _token_mixer(x, w_in, pool_w, pool_scale, conv_dw, conv_ln_g, conv_ln_b, conv_w_out, w_out):
    b, s, _ = x.shape
    proj = jnp.einsum('bsd,dc->bsc', x, w_in)
    o_conv = POOL_WIDTH
    o_gate = POOL_WIDTH + 2 * CONV_WIDTH
    u = proj[..., :POOL_WIDTH].reshape(b, s, POOL_GROUPS, POOL_GROUP_IN)
    r = jnp.stack([_causal_pool_residual(u[:, :, g], POOL_WINDOWS[g]) for g in range(POOL_GROUPS)], axis=2)
    y_pool = jnp.einsum('bsgc,gcd->bsgd', r, pool_w).reshape(b, s, D_MODEL) * pool_scale
    a = proj[..., o_conv:o_gate]
    v = a[..., :CONV_WIDTH] * jax.nn.sigmoid(a[..., CONV_WIDTH:])
    v = lax.conv_general_dilated(v, conv_dw[:, None, :], window_strides=(1,),
                                 padding=[(CONV_KERNEL - 1, 0)],
                                 dimension_numbers=('NWC', 'WIO', 'NWC'),
                                 feature_group_count=CONV_WIDTH)
    v = jax.nn.silu(_layernorm(v, conv_ln_g, conv_ln_b))
    y_conv = jnp.einsum('bsc,cd->bsd', v, conv_w_out)
    g_pool = jax.nn.sigmoid(proj[..., o_gate:o_gate + D_MODEL])
    g_conv = jax.nn.sigmoid(proj[..., o_gate + D_MODEL:])
    merged = g_pool * y_pool + g_conv * y_conv
    return jnp.einsum('bsd,de->bse', merged, w_out)


def _route(x2d, w_router, router_bias):
    t = x2d.shape[0]
    scores = jax.nn.sigmoid(jnp.dot(x2d.astype(jnp.float32), w_router.astype(jnp.float32)))
    sel = scores + router_bias.astype(jnp.float32)
    group_score = lax.top_k(sel.reshape(t, N_GROUPS, EXPERTS_PER_GROUP), 2)[0].sum(-1)
    _, top_groups = lax.top_k(group_score, TOPK_GROUPS)
    group_mask = jax.nn.one_hot(top_groups, N_GROUPS, dtype=jnp.bool_).any(axis=1)
    expert_mask = jnp.repeat(group_mask, EXPERTS_PER_GROUP, axis=1)
    _, topk_idx = lax.top_k(jnp.where(expert_mask, sel, -jnp.inf), TOP_K)
    w = jnp.take_along_axis(scores, topk_idx, axis=1)
    w = w / jnp.sum(w, axis=-1, keepdims=True) * ROUTED_SCALE
    return topk_idx, w


def _moe(x, w_router, router_bias, w_gate_e, w_up_e, w_down_e, ws_gate, ws_up, ws_down):
    b, s, d = x.shape
    x2d = x.reshape(b * s, d)
    t = x2d.shape[0]
    topk_idx, w = _route(x2d, w_router, router_bias)
    n = t * TOP_K
    flat_e = topk_idx.reshape(n)
    flat_tok = (jnp.arange(n) // TOP_K).astype(jnp.int32)
    flat_w = w.reshape(n).astype(x.dtype)
    order = jnp.argsort(flat_e)
    counts = jnp.bincount(flat_e, length=N_EXPERTS)
    padded = ((counts + EXPERT_BLOCK - 1) // EXPERT_BLOCK) * EXPERT_BLOCK
    pad_end = jnp.cumsum(padded)
    pad_start = pad_end - padded
    start = jnp.cumsum(counts) - counts
    se = flat_e[order]
    dest = pad_start[se] + jnp.arange(n) - start[se]
    n_blocks = -(-(n + N_EXPERTS * (EXPERT_BLOCK - 1)) // EXPERT_BLOCK)
    n_pad = n_blocks * EXPERT_BLOCK
    slot_tok = jnp.full((n_pad,), t, dtype=jnp.int32).at[dest].set(flat_tok[order])
    slot_w = jnp.zeros((n_pad,), dtype=x.dtype).at[dest].set(flat_w[order])
    block_e = jnp.minimum(jnp.searchsorted(pad_end, jnp.arange(n_blocks) * EXPERT_BLOCK, side='right'),
                          N_EXPERTS - 1)
    x_pad = jnp.concatenate([x2d, jnp.zeros((1, d), x2d.dtype)], axis=0)

    def expert_block(y, inp):
        tok, wt, e = inp
        xb = x_pad[tok]
        h = jax.nn.silu(xb @ w_gate_e[e]) * (xb @ w_up_e[e])
        yb = (h @ w_down_e[e]) * wt[:, None]
        return y.at[tok].add(yb), None

    y_routed, _ = lax.scan(expert_block, jnp.zeros((t + 1, d), x2d.dtype),
                           (slot_tok.reshape(n_blocks, EXPERT_BLOCK),
                            slot_w.reshape(n_blocks, EXPERT_BLOCK), block_e))
    y_shared = (jax.nn.silu(x2d @ ws_gate) * (x2d @ ws_up)) @ ws_down
    return (y_routed[:t] + y_shared).reshape(b, s, d)


def setup_inputs(seed: int = 0) -> dict:
    key = jax.random.key(seed)
    ks = jax.random.split(key, 24)
    f32 = jnp.float32

    def nrm(k, shape, scale):
        return jax.random.normal(k, shape, f32) * scale

    L = DEPTH
    return {
        "x": nrm(ks[0], (BATCH, SEQ, D_MODEL), 1.0),
        "w_in": nrm(ks[1], (L, D_MODEL, IN_COLS), D_MODEL ** -0.5),
        "pool_w": nrm(ks[2], (L, POOL_GROUPS, POOL_GROUP_IN, POOL_GROUP_OUT), POOL_GROUP_IN ** -0.5),
        "pool_scale": 1.0 + nrm(ks[3], (L, D_MODEL), 0.1),
        "conv_dw": nrm(ks[4], (L, CONV_KERNEL, CONV_WIDTH), CONV_KERNEL ** -0.5),
        "conv_ln_g": 1.0 + nrm(ks[5], (L, CONV_WIDTH), 0.02),
        "conv_ln_b": nrm(ks[6], (L, CONV_WIDTH), 0.02),
        "conv_w_out": nrm(ks[7], (L, CONV_WIDTH, D_MODEL), CONV_WIDTH ** -0.5),
        "w_out": nrm(ks[8], (L, D_MODEL, D_MODEL), D_MODEL ** -0.5 * DEEPNORM_BETA),
        "ln1_g": 1.0 + nrm(ks[9], (L, D_MODEL), 0.02),
        "ln1_b": nrm(ks[10], (L, D_MODEL), 0.02),
        "w_router": nrm(ks[11], (L, D_MODEL, N_EXPERTS), D_MODEL ** -0.5),
        "router_bias": nrm(ks[12], (L, N_EXPERTS), 0.01),
        "w_gate_e": nrm(ks[13], (L, N_EXPERTS, D_MODEL, EXPERT_HIDDEN), D_MODEL ** -0.5),
        "w_up_e": nrm(ks[14], (L, N_EXPERTS, D_MODEL, EXPERT_HIDDEN), D_MODEL ** -0.5),
        "w_down_e": nrm(ks[15], (L, N_EXPERTS, EXPERT_HIDDEN, D_MODEL), EXPERT_HIDDEN ** -0.5 * DEEPNORM_BETA),
        "ws_gate": nrm(ks[16], (L, D_MODEL, SHARED_HIDDEN), D_MODEL ** -0.5),
        "ws_up": nrm(ks[17], (L, D_MODEL, SHARED_HIDDEN), D_MODEL ** -0.5),
        "ws_down": nrm(ks[18], (L, SHARED_HIDDEN, D_MODEL), SHARED_HIDDEN ** -0.5 * DEEPNORM_BETA),
        "ln2_g": 1.0 + nrm(ks[19], (L, D_MODEL), 0.02),
        "ln2_b": nrm(ks[20], (L, D_MODEL), 0.02),
    }


def reference(x, w_in, pool_w, pool_scale, conv_dw, conv_ln_g, conv_ln_b, conv_w_out, w_out,
              ln1_g, ln1_b, w_router, router_bias, w_gate_e, w_up_e, w_down_e,
              ws_gate, ws_up, ws_down, ln2_g, ln2_b):
    for l in range(DEPTH):
        m = _token_mixer(x, w_in[l], pool_w[l], pool_scale[l], conv_dw[l], conv_ln_g[l],
                         conv_ln_b[l], conv_w_out[l], w_out[l])
        x = _layernorm(DEEPNORM_ALPHA * x + m, ln1_g[l], ln1_b[l])
        f = _moe(x, w_router[l], router_bias[l], w_gate_e[l], w_up_e[l], w_down_e[l],
                 ws_gate[l], ws_up[l], ws_down[l])
        x = _layernorm(DEEPNORM_ALPHA * x + f, ln2_g[l], ln2_b[l])
    return x
```

```python
import functools

import jax
import jax.numpy as jnp
from jax import lax
from jax.experimental import pallas as pl
from jax.experimental.pallas import tpu as pltpu

F32 = jnp.float32
BF16 = jnp.bfloat16
I32 = jnp.int32

POOL_GROUPS = 4
POOL_WINDOWS = (2, 4, 8, 16)
CONV_KERNEL = 31
N_EXPERTS = 256
TOP_K = 8
N_GROUPS = 8
TOPK_GROUPS = 4
EXPERTS_PER_GROUP = N_EXPERTS // N_GROUPS
ROUTED_SCALE = 2.5
EXPERT_BLOCK = 128
LN_EPS = 1e-5

MIX_ROWS = 512
MIX_CHUNK = 32
POOL_HIST = 16
CONV_HIST = 32
ROUTE_COLS = 512
MOVE_ROWS = 256
VMEM_LIMIT_MIXER = 56 * 1024 * 1024
VMEM_LIMIT_OTHER = 40 * 1024 * 1024


def _layernorm(z, g, b):
    mu = jnp.mean(z, axis=-1, keepdims=True)
    d = z - mu
    var = jnp.mean(d * d, axis=-1, keepdims=True)
    return d * lax.rsqrt(var + LN_EPS) * g + b


def _silu(v):
    return v * jax.nn.sigmoid(v)


def _mixer_kernel(x_ref, w_in_ref, pool_w_ref, pool_scale_ref, dw_ref, cg_ref, cb_ref, cwo_ref,
                  w_out_ref, g1_ref, b1_ref, wr_ref, x1_ref, lt_ref,
                  ubuf, vbuf, rbuf, cbuf, *, alpha):
    ts = x_ref.shape[1]
    d_model = x_ref.shape[2]
    pw = ubuf.shape[1]
    cw = vbuf.shape[1]
    gi = pw // POOL_GROUPS
    si = pl.program_id(1)

    @pl.when(si == 0)
    def _():
        ubuf[0:POOL_HIST, :] = jnp.zeros((POOL_HIST, pw), F32)
        vbuf[0:CONV_HIST, :] = jnp.zeros((CONV_HIST, cw), F32)

    x = x_ref[0]
    xb = x.astype(BF16)
    ubuf[POOL_HIST:POOL_HIST + ts, :] = jnp.dot(xb, w_in_ref[:, 0:pw], preferred_element_type=F32)
    a = jnp.dot(xb, w_in_ref[:, pw:pw + 2 * cw], preferred_element_type=F32)
    vbuf[CONV_HIST:CONV_HIST + ts, :] = a[:, :cw] * jax.nn.sigmoid(a[:, cw:])
    pos0 = si * ts

    for c in range(ts // MIX_CHUNK):
        r0 = c * MIX_CHUNK
        pos = pos0 + r0 + lax.broadcasted_iota(I32, (MIX_CHUNK, 1), 0)
        for g, w in enumerate(POOL_WINDOWS):
            cols = slice(g * gi, (g + 1) * gi)
            cur = ubuf[pl.ds(POOL_HIST + r0, MIX_CHUNK), cols]
            s = cur
            for j in range(1, w):
                s = s + ubuf[pl.ds(POOL_HIST + r0 - j, MIX_CHUNK), cols]
            cnt = jnp.minimum(pos + 1, w).astype(F32)
            rbuf[pl.ds(r0, MIX_CHUNK), cols] = (s / cnt - cur).astype(BF16)
        base = CONV_HIST + r0 - (CONV_KERNEL - 1)
        acc = dw_ref[0:1, :] * vbuf[pl.ds(base, MIX_CHUNK), :]
        for k in range(1, CONV_KERNEL):
            acc = acc + dw_ref[k:k + 1, :] * vbuf[pl.ds(base + k, MIX_CHUNK), :]
        n = _layernorm(acc, cg_ref[...], cb_ref[...])
        cbuf[pl.ds(r0, MIX_CHUNK), :] = _silu(n).astype(BF16)

    ubuf[0:POOL_HIST, :] = ubuf[ts:ts + POOL_HIST, :]
    vbuf[0:CONV_HIST, :] = vbuf[ts:ts + CONV_HIST, :]

    y_pool = jnp.concatenate(
        [jnp.dot(rbuf[:, g * gi:(g + 1) * gi], pool_w_ref[g], preferred_element_type=F32)
         for g in range(POOL_GROUPS)], axis=1) * pool_scale_ref[...]
    y_conv = jnp.dot(cbuf[...], cwo_ref[...], preferred_element_type=F32)
    gates = jnp.dot(xb, w_in_ref[:, pw + 2 * cw:], preferred_element_type=F32)
    merged = (jax.nn.sigmoid(gates[:, :d_model]) * y_pool
              + jax.nn.sigmoid(gates[:, d_model:]) * y_conv)
    m = jnp.dot(merged.astype(BF16), w_out_ref[...], preferred_element_type=F32)
    x1 = _layernorm(alpha * x + m, g1_ref[...], b1_ref[...])
    x1_ref[...] = x1
    lt_ref[...] = lax.dot_general(wr_ref[...], x1.astype(BF16), (((1,), (1,)), ((), ())),
                                  preferred_element_type=F32)


def _mixer_call(x, w_in, pool_w, pool_scale, conv_dw, cln_g, cln_b, conv_w_out, w_out,
                ln1_g, ln1_b, w_router_t, *, alpha):
    b, s, d = x.shape
    ts = MIX_ROWS
    ns = s // ts
    pw = pool_w.shape[0] * pool_w.shape[1]
    cw = conv_dw.shape[1]
    n_exp = w_router_t.shape[0]

    def const(shape):
        return pl.BlockSpec(shape, lambda bi, si: (0,) * len(shape))

    return pl.pallas_call(
        functools.partial(_mixer_kernel, alpha=alpha),
        grid=(b, ns),
        in_specs=[
            pl.BlockSpec((1, ts, d), lambda bi, si: (bi, si, 0)),
            const(w_in.shape), const(pool_w.shape), const(pool_scale.shape), const(conv_dw.shape),
            const(cln_g.shape), const(cln_b.shape), const(conv_w_out.shape), const(w_out.shape),
            const(ln1_g.shape), const(ln1_b.shape), const(w_router_t.shape),
        ],
        out_specs=[
            pl.BlockSpec((ts, d), lambda bi, si: (bi * ns + si, 0)),
            pl.BlockSpec((n_exp, ts), lambda bi, si: (0, bi * ns + si)),
        ],
        out_shape=[jax.ShapeDtypeStruct((b * s, d), F32),
                   jax.ShapeDtypeStruct((n_exp, b * s), F32)],
        scratch_shapes=[
            pltpu.VMEM((POOL_HIST + ts, pw), F32),
            pltpu.VMEM((CONV_HIST + ts, cw), F32),
            pltpu.VMEM((ts, pw), BF16),
            pltpu.VMEM((ts, cw), BF16),
        ],
        compiler_params=pltpu.CompilerParams(
            dimension_semantics=("arbitrary", "arbitrary"), vmem_limit_bytes=VMEM_LIMIT_MIXER),
        name="mixer",
    )(x, w_in, pool_w, pool_scale, conv_dw, cln_g, cln_b, conv_w_out, w_out, ln1_g, ln1_b,
      w_router_t)


def _route_kernel(lt_ref, bias_ref, idx_ref, w_ref, rank_ref, cnt_ref, carry):
    n_exp, tt = lt_ref.shape
    neg = -jnp.inf

    @pl.when(pl.program_id(0) == 0)
    def _():
        carry[...] = jnp.zeros_like(carry)

    scores = jax.nn.sigmoid(lt_ref[...])
    sel = scores + bias_ref[...]
    sel3 = sel.reshape(N_GROUPS, EXPERTS_PER_GROUP, tt)
    io3 = lax.broadcasted_iota(I32, sel3.shape, 1)
    m1 = jnp.max(sel3, axis=1, keepdims=True)
    i1 = jnp.min(jnp.where(sel3 == m1, io3, EXPERTS_PER_GROUP), axis=1, keepdims=True)
    m2 = jnp.max(jnp.where(io3 == i1, neg, sel3), axis=1, keepdims=True)
    gscore = m1 + m2
    iog = lax.broadcasted_iota(I32, gscore.shape, 0)
    gsel = jnp.zeros(gscore.shape, F32)
    for _ in range(TOPK_GROUPS):
        m = jnp.max(gscore, axis=0, keepdims=True)
        gi = jnp.min(jnp.where(gscore == m, iog, N_GROUPS), axis=0, keepdims=True)
        hit = iog == gi
        gsel = jnp.where(hit, 1.0, gsel)
        gscore = jnp.where(hit, neg, gscore)
    val = jnp.where(gsel > 0.5, sel3, neg).reshape(n_exp, tt)
    ioe = lax.broadcasted_iota(I32, (n_exp, tt), 0)
    member = jnp.zeros((n_exp, tt), F32)
    idxs, ws = [], []
    for _ in range(TOP_K):
        m = jnp.max(val, axis=0, keepdims=True)
        ei = jnp.min(jnp.where(val == m, ioe, n_exp), axis=0, keepdims=True)
        hit = ioe == ei
        idxs.append(ei)
        ws.append(jnp.sum(jnp.where(hit, scores, 0.0), axis=0, keepdims=True))
        member = jnp.where(hit, 1.0, member)
        val = jnp.where(hit, neg, val)
    w = jnp.concatenate(ws, axis=0)
    w_ref[...] = w / jnp.sum(w, axis=0, keepdims=True) * ROUTED_SCALE
    idx_ref[...] = jnp.concatenate(idxs, axis=0)
    before = (lax.broadcasted_iota(I32, (tt, tt), 0) < lax.broadcasted_iota(I32, (tt, tt), 1))
    excl = jnp.dot(member.astype(BF16), before.astype(BF16), preferred_element_type=F32) + carry[...]
    ranks = [jnp.sum(jnp.where(ioe == ei, excl, 0.0), axis=0, keepdims=True) for ei in idxs]
    rank_ref[...] = jnp.concatenate(ranks, axis=0).astype(I32)
    carry[...] = carry[...] + jnp.sum(member, axis=1, keepdims=True)
    cnt_ref[...] = carry[...]


def _route_call(logits_t, bias_col):
    n_exp, t = logits_t.shape
    tt = ROUTE_COLS
    return pl.pallas_call(
        _route_kernel,
        grid=(t // tt,),
        in_specs=[pl.BlockSpec((n_exp, tt), lambda i: (0, i)),
                  pl.BlockSpec((n_exp, 1), lambda i: (0, 0))],
        out_specs=[pl.BlockSpec((TOP_K, tt), lambda i: (0, i)),
                   pl.BlockSpec((TOP_K, tt), lambda i: (0, i)),
                   pl.BlockSpec((TOP_K, tt), lambda i: (0, i)),
                   pl.BlockSpec((n_exp, 1), lambda i: (0, 0))],
        out_shape=[jax.ShapeDtypeStruct((TOP_K, t), I32),
                   jax.ShapeDtypeStruct((TOP_K, t), F32),
                   jax.ShapeDtypeStruct((TOP_K, t), I32),
                   jax.ShapeDtypeStruct((n_exp, 1), F32)],
        scratch_shapes=[pltpu.VMEM((n_exp, 1), F32)],
        compiler_params=pltpu.CompilerParams(
            dimension_semantics=("arbitrary",), vmem_limit_bytes=VMEM_LIMIT_OTHER),
        name="route",
    )(logits_t, bias_col)


def _dispatch_kernel(zflag_ref, dest_ref, x1_ref, xs_hbm, zbuf, zsem, sem):
    tk = x1_ref.shape[0]
    n_blocks = zflag_ref.shape[0]

    def zero_copy(blk):
        return pltpu.make_async_copy(
            zbuf, xs_hbm.at[pl.ds(pl.multiple_of(blk * EXPERT_BLOCK, EXPERT_BLOCK), EXPERT_BLOCK)],
            zsem)

    @pl.when(pl.program_id(0) == 0)
    def _():
        zbuf[...] = jnp.zeros_like(zbuf)

        def start(blk, c):
            @pl.when(zflag_ref[blk] == 1)
            def _():
                zero_copy(blk).start()
            return c

        def wait(blk, c):
            @pl.when(zflag_ref[blk] == 1)
            def _():
                zero_copy(blk).wait()
            return c

        lax.fori_loop(0, n_blocks, start, 0)
        lax.fori_loop(0, n_blocks, wait, 0)

    def issue(t, c):
        for k in range(TOP_K):
            pltpu.make_async_copy(x1_ref.at[pl.ds(t, 1)], xs_hbm.at[pl.ds(dest_ref[k, t], 1)],
                                  sem).start()
        return c

    lax.fori_loop(0, tk, issue, 0)
    for _ in range(TOP_K):
        pltpu.make_async_copy(x1_ref, xs_hbm.at[pl.ds(0, tk)], sem).wait()


def _dispatch_call(zflag, dest, x1, n_pad):
    t, d = x1.shape
    tk = MOVE_ROWS
    grid_spec = pltpu.PrefetchScalarGridSpec(
        num_scalar_prefetch=1,
        grid=(t // tk,),
        in_specs=[pl.BlockSpec((TOP_K, tk), lambda i, zf: (0, i), memory_space=pltpu.SMEM),
                  pl.BlockSpec((tk, d), lambda i, zf: (i, 0))],
        out_specs=pl.BlockSpec(memory_space=pl.ANY),
        scratch_shapes=[pltpu.VMEM((EXPERT_BLOCK, d), F32),
                        pltpu.SemaphoreType.DMA(()),
                        pltpu.SemaphoreType.DMA(())],
    )
    return pl.pallas_call(
        _dispatch_kernel,
        grid_spec=grid_spec,
        out_shape=jax.ShapeDtypeStruct((n_pad, d), F32),
        compiler_params=pltpu.CompilerParams(
            dimension_semantics=("arbitrary",), vmem_limit_bytes=VMEM_LIMIT_OTHER),
        name="dispatch",
    )(zflag, dest, x1)


def _expert_kernel(be_ref, nu_ref, xs_ref, wg_ref, wu_ref, wd_ref, ys_ref, wgu_s, wd_s):
    i = pl.program_id(0)
    hid = wg_ref.shape[2]
    e = be_ref[i]
    prev = be_ref[jnp.maximum(i - 1, 0)]

    @pl.when((i == 0) | (e != prev))
    def _():
        wgu_s[:, 0:hid] = wg_ref[0].astype(BF16)
        wgu_s[:, hid:2 * hid] = wu_ref[0].astype(BF16)
        wd_s[...] = wd_ref[0].astype(BF16)

    @pl.when(i < nu_ref[0])
    def _():
        gu = jnp.dot(xs_ref[...].astype(BF16), wgu_s[...], preferred_element_type=F32)
        h = _silu(gu[:, :hid]) * gu[:, hid:]
        ys_ref[...] = jnp.dot(h.astype(BF16), wd_s[...], preferred_element_type=F32)

    @pl.when(i >= nu_ref[0])
    def _():
        ys_ref[...] = jnp.zeros_like(ys_ref)


def _expert_call(block_e, n_used, xs, w_gate_e, w_up_e, w_down_e):
    n_pad, d = xs.shape
    n_blocks = n_pad // EXPERT_BLOCK
    hid = w_gate_e.shape[2]

    def xs_map(i, be, nu):
        return (jnp.minimum(i, jnp.maximum(nu[0] - 1, 0)), 0)

    grid_spec = pltpu.PrefetchScalarGridSpec(
        num_scalar_prefetch=2,
        grid=(n_blocks,),
        in_specs=[pl.BlockSpec((EXPERT_BLOCK, d), xs_map),
                  pl.BlockSpec((1, d, hid), lambda i, be, nu: (be[i], 0, 0)),
                  pl.BlockSpec((1, d, hid), lambda i, be, nu: (be[i], 0, 0)),
                  pl.BlockSpec((1, hid, d), lambda i, be, nu: (be[i], 0, 0))],
        out_specs=pl.BlockSpec((EXPERT_BLOCK, d), lambda i, be, nu: (i, 0)),
        scratch_shapes=[pltpu.VMEM((d, 2 * hid), BF16), pltpu.VMEM((hid, d), BF16)],
    )
    return pl.pallas_call(
        _expert_kernel,
        grid_spec=grid_spec,
        out_shape=jax.ShapeDtypeStruct((n_pad, d), F32),
        compiler_params=pltpu.CompilerParams(
            dimension_semantics=("arbitrary",), vmem_limit_bytes=VMEM_LIMIT_OTHER),
        name="experts",
    )(block_e, n_used, xs, w_gate_e, w_up_e, w_down_e)


def _combine_kernel(dest_ref, x1_ref, w_ref, ys_hbm, wsgu_ref, wsd_ref, g2_ref, b2_ref, out_ref,
                    buf, sem, *, alpha):
    tk = x1_ref.shape[0]
    hid = wsd_ref.shape[0]

    def issue(t, c):
        for k in range(TOP_K):
            pltpu.make_async_copy(ys_hbm.at[pl.ds(dest_ref[k, t], 1)], buf.at[k, pl.ds(t, 1)],
                                  sem).start()
        return c

    lax.fori_loop(0, tk, issue, 0)
    x1 = x1_ref[...]
    gu = jnp.dot(x1.astype(BF16), wsgu_ref[...], preferred_element_type=F32)
    hs = _silu(gu[:, :hid]) * gu[:, hid:]
    f = jnp.dot(hs.astype(BF16), wsd_ref[...], preferred_element_type=F32)
    for k in range(TOP_K):
        pltpu.make_async_copy(ys_hbm.at[pl.ds(0, tk)], buf.at[k], sem).wait()
    w = w_ref[...]
    for k in range(TOP_K):
        f = f + w[:, k:k + 1] * buf[k]
    out_ref[...] = _layernorm(alpha * x1 + f, g2_ref[...], b2_ref[...])


def _combine_call(dest, x1, w_tok, ys, ws_gu, ws_d, ln2_g, ln2_b, *, alpha):
    t, d = x1.shape
    tk = MOVE_ROWS

    def const(shape):
        return pl.BlockSpec(shape, lambda i: (0,) * len(shape))

    return pl.pallas_call(
        functools.partial(_combine_kernel, alpha=alpha),
        grid=(t // tk,),
        in_specs=[pl.BlockSpec((TOP_K, tk), lambda i: (0, i), memory_space=pltpu.SMEM),
                  pl.BlockSpec((tk, d), lambda i: (i, 0)),
                  pl.BlockSpec((tk, TOP_K), lambda i: (i, 0)),
                  pl.BlockSpec(memory_space=pl.ANY),
                  const(ws_gu.shape), const(ws_d.shape), const(ln2_g.shape), const(ln2_b.shape)],
        out_specs=pl.BlockSpec((tk, d), lambda i: (i, 0)),
        out_shape=jax.ShapeDtypeStruct((t, d), F32),
        scratch_shapes=[pltpu.VMEM((TOP_K, tk, d), F32), pltpu.SemaphoreType.DMA(())],
        compiler_params=pltpu.CompilerParams(
            dimension_semantics=("arbitrary",), vmem_limit_bytes=VMEM_LIMIT_OTHER),
        name="combine",
    )(dest, x1, w_tok, ys, ws_gu, ws_d, ln2_g, ln2_b)


def _layer(x, w_in, pool_w, pool_scale, conv_dw, conv_ln_g, conv_ln_b, conv_w_out, w_out,
           ln1_g, ln1_b, w_router, router_bias, w_gate_e, w_up_e, w_down_e,
           ws_gate, ws_up, ws_down, ln2_g, ln2_b, *, alpha):
    b, s, d = x.shape
    t = b * s
    row = lambda v: v.reshape(1, -1)
    x1, logits_t = _mixer_call(
        x, w_in.astype(BF16), pool_w.astype(BF16), row(pool_scale), conv_dw, row(conv_ln_g),
        row(conv_ln_b), conv_w_out.astype(BF16), w_out.astype(BF16), row(ln1_g), row(ln1_b),
        w_router.T.astype(BF16), alpha=alpha)
    idx_t, w_t, rank_t, counts = _route_call(logits_t, router_bias.reshape(-1, 1))

    n = t * TOP_K
    n_blocks = -(-(n + N_EXPERTS * (EXPERT_BLOCK - 1)) // EXPERT_BLOCK)
    n_pad = n_blocks * EXPERT_BLOCK
    counts = counts[:, 0].astype(I32)
    padded = ((counts + EXPERT_BLOCK - 1) // EXPERT_BLOCK) * EXPERT_BLOCK
    pad_end = jnp.cumsum(padded)
    pad_start = pad_end - padded
    n_used = (pad_end[-1] // EXPERT_BLOCK).astype(I32)
    blk = jnp.arange(n_blocks, dtype=I32)
    block_e = jnp.minimum(jnp.searchsorted(pad_end, blk * EXPERT_BLOCK, side='right'),
                          N_EXPERTS - 1).astype(I32)
    block_e = jnp.where(blk < n_used, block_e, block_e[jnp.maximum(n_used - 1, 0)])
    nxt = jnp.concatenate([block_e[1:], jnp.full((1,), -1, I32)])
    zflag = ((blk >= n_used - 1) | (nxt != block_e)).astype(I32)
    dest = (pad_start[idx_t] + rank_t).astype(I32)

    xs = _dispatch_call(zflag, dest, x1, n_pad)
    ys = _expert_call(block_e, n_used.reshape(1), xs, w_gate_e, w_up_e, w_down_e)
    ws_gu = jnp.concatenate([ws_gate, ws_up], axis=1).astype(BF16)
    out = _combine_call(dest, x1, w_t.T, ys, ws_gu, ws_down.astype(BF16), row(ln2_g), row(ln2_b),
                        alpha=alpha)
    return out.reshape(b, s, d)


def kernel(x, w_in, pool_w, pool_scale, conv_dw, conv_ln_g, conv_ln_b, conv_w_out, w_out, ln1_g, ln1_b, w_router, router_bias, w_gate_e, w_up_e, w_down_e, ws_gate, ws_up, ws_down, ln2_g, ln2_b):
    depth = w_in.shape[0]
    alpha = (2.0 * depth) ** 0.25
    for l in range(depth):
        x = _layer(x, w_in[l], pool_w[l], pool_scale[l], conv_dw[l], conv_ln_g[l], conv_ln_b[l],
                   conv_w_out[l], w_out[l], ln1_g[l], ln1_b[l], w_router[l], router_bias[l],
                   w_gate_e[l], w_up_e[l], w_down_e[l], ws_gate[l], ws_up[l], ws_down[l],
                   ln2_g[l], ln2_b[l], alpha=alpha)
    return x
```

```python
import functools

import jax
import jax.numpy as jnp
from jax import lax
from jax.experimental import pallas as pl
from jax.experimental.pallas import tpu as pltpu

F32 = jnp.float32
BF16 = jnp.bfloat16
I32 = jnp.int32

POOL_GROUPS = 4
POOL_WINDOWS = (2, 4, 8, 16)
CONV_KERNEL = 31
N_EXPERTS = 256
TOP_K = 8
N_GROUPS = 8
TOPK_GROUPS = 4
EXPERTS_PER_GROUP = N_EXPERTS // N_GROUPS
ROUTED_SCALE = 2.5
EXPERT_BLOCK = 256
LN_EPS = 1e-5

MIX_ROWS = 512
MIX_CHUNK = 32
POOL_HIST = 16
CONV_HIST = 32
ROUTE_COLS = 512
MOVE_ROWS = 256
VMEM_LIMIT_MIXER = 56 * 1024 * 1024
VMEM_LIMIT_OTHER = 40 * 1024 * 1024


def _layernorm(z, g, b):
    mu = jnp.mean(z, axis=-1, keepdims=True)
    d = z - mu
    var = jnp.mean(d * d, axis=-1, keepdims=True)
    return d * lax.rsqrt(var + LN_EPS) * g + b


def _silu(v):
    return v * jax.nn.sigmoid(v)


def _mixer_kernel(x_ref, w_in_ref, pool_w_ref, pool_scale_ref, dw_ref, cg_ref, cb_ref, cwo_ref,
                  w_out_ref, g1_ref, b1_ref, wr_ref, x1_ref, lt_ref,
                  ubuf, vbuf, rbuf, cbuf, *, alpha):
    ts = x_ref.shape[1]
    d_model = x_ref.shape[2]
    pw = ubuf.shape[1]
    cw = vbuf.shape[1]
    gi = pw // POOL_GROUPS
    si = pl.program_id(1)

    @pl.when(si == 0)
    def _():
        ubuf[0:POOL_HIST, :] = jnp.zeros((POOL_HIST, pw), F32)
        vbuf[0:CONV_HIST, :] = jnp.zeros((CONV_HIST, cw), F32)

    x = x_ref[0]
    xb = x.astype(BF16)
    ubuf[POOL_HIST:POOL_HIST + ts, :] = jnp.dot(xb, w_in_ref[:, 0:pw], preferred_element_type=F32)
    a = jnp.dot(xb, w_in_ref[:, pw:pw + 2 * cw], preferred_element_type=F32)
    vbuf[CONV_HIST:CONV_HIST + ts, :] = a[:, :cw] * jax.nn.sigmoid(a[:, cw:])
    pos0 = si * ts

    for c in range(ts // MIX_CHUNK):
        r0 = c * MIX_CHUNK
        pos = pos0 + r0 + lax.broadcasted_iota(I32, (MIX_CHUNK, 1), 0)
        for g, w in enumerate(POOL_WINDOWS):
            cols = slice(g * gi, (g + 1) * gi)
            cur = ubuf[pl.ds(POOL_HIST + r0, MIX_CHUNK), cols]
            s = cur
            for j in range(1, w):
                s = s + ubuf[pl.ds(POOL_HIST + r0 - j, MIX_CHUNK), cols]
            cnt = jnp.minimum(pos + 1, w).astype(F32)
            rbuf[pl.ds(r0, MIX_CHUNK), cols] = (s / cnt - cur).astype(BF16)
        base = CONV_HIST + r0 - (CONV_KERNEL - 1)
        acc = dw_ref[0:1, :] * vbuf[pl.ds(base, MIX_CHUNK), :]
        for k in range(1, CONV_KERNEL):
            acc = acc + dw_ref[k:k + 1, :] * vbuf[pl.ds(base + k, MIX_CHUNK), :]
        n = _layernorm(acc, cg_ref[...], cb_ref[...])
        cbuf[pl.ds(r0, MIX_CHUNK), :] = _silu(n).astype(BF16)

    ubuf[0:POOL_HIST, :] = ubuf[ts:ts + POOL_HIST, :]
    vbuf[0:CONV_HIST, :] = vbuf[ts:ts + CONV_HIST, :]

    y_pool = jnp.concatenate(
        [jnp.dot(rbuf[:, g * gi:(g + 1) * gi], pool_w_ref[g], preferred_element_type=F32)
         for g in range(POOL_GROUPS)], axis=1) * pool_scale_ref[...]
    y_conv = jnp.dot(cbuf[...], cwo_ref[...], preferred_element_type=F32)
    gates = jnp.dot(xb, w_in_ref[:, pw + 2 * cw:], preferred_element_type=F32)
    merged = (jax.nn.sigmoid(gates[:, :d_model]) * y_pool
              + jax.nn.sigmoid(gates[:, d_model:]) * y_conv)
    m = jnp.dot(merged.astype(BF16), w_out_ref[...], preferred_element_type=F32)
    x1 = _layernorm(alpha * x + m, g1_ref[...], b1_ref[...])
    x1_ref[...] = x1
    lt_ref[...] = lax.dot_general(wr_ref[...], x1.astype(BF16), (((1,), (1,)), ((), ())),
                                  preferred_element_type=F32)


def _mixer_call(x, w_in, pool_w, pool_scale, conv_dw, cln_g, cln_b, conv_w_out, w_out,
                ln1_g, ln1_b, w_router_t, *, alpha):
    b, s, d = x.shape
    ts = MIX_ROWS
    ns = s // ts
    pw = pool_w.shape[0] * pool_w.shape[1]
    cw = conv_dw.shape[1]
    n_exp = w_router_t.shape[0]

    def const(shape):
        return pl.BlockSpec(shape, lambda bi, si: (0,) * len(shape))

    return pl.pallas_call(
        functools.partial(_mixer_kernel, alpha=alpha),
        grid=(b, ns),
        in_specs=[
            pl.BlockSpec((1, ts, d), lambda bi, si: (bi, si, 0)),
            const(w_in.shape), const(pool_w.shape), const(pool_scale.shape), const(conv_dw.shape),
            const(cln_g.shape), const(cln_b.shape), const(conv_w_out.shape), const(w_out.shape),
            const(ln1_g.shape), const(ln1_b.shape), const(w_router_t.shape),
        ],
        out_specs=[
            pl.BlockSpec((ts, d), lambda bi, si: (bi * ns + si, 0)),
            pl.BlockSpec((n_exp, ts), lambda bi, si: (0, bi * ns + si)),
        ],
        out_shape=[jax.ShapeDtypeStruct((b * s, d), F32),
                   jax.ShapeDtypeStruct((n_exp, b * s), F32)],
        scratch_shapes=[
            pltpu.VMEM((POOL_HIST + ts, pw), F32),
            pltpu.VMEM((CONV_HIST + ts, cw), F32),
            pltpu.VMEM((ts, pw), BF16),
            pltpu.VMEM((ts, cw), BF16),
        ],
        compiler_params=pltpu.CompilerParams(
            dimension_semantics=("arbitrary", "arbitrary"), vmem_limit_bytes=VMEM_LIMIT_MIXER),
        name="mixer",
    )(x, w_in, pool_w, pool_scale, conv_dw, cln_g, cln_b, conv_w_out, w_out, ln1_g, ln1_b,
      w_router_t)


def _route_kernel(lt_ref, bias_ref, idx_ref, w_ref, rank_ref, cnt_ref, carry):
    n_exp, tt = lt_ref.shape
    neg = -jnp.inf

    @pl.when(pl.program_id(0) == 0)
    def _():
        carry[...] = jnp.zeros_like(carry)

    scores = jax.nn.sigmoid(lt_ref[...])
    sel = scores + bias_ref[...]
    sel3 = sel.reshape(N_GROUPS, EXPERTS_PER_GROUP, tt)
    io3 = lax.broadcasted_iota(I32, sel3.shape, 1)
    m1 = jnp.max(sel3, axis=1, keepdims=True)
    i1 = jnp.min(jnp.where(sel3 == m1, io3, EXPERTS_PER_GROUP), axis=1, keepdims=True)
    m2 = jnp.max(jnp.where(io3 == i1, neg, sel3), axis=1, keepdims=True)
    gscore = m1 + m2
    iog = lax.broadcasted_iota(I32, gscore.shape, 0)
    gsel = jnp.zeros(gscore.shape, F32)
    for _ in range(TOPK_GROUPS):
        m = jnp.max(gscore, axis=0, keepdims=True)
        gi = jnp.min(jnp.where(gscore == m, iog, N_GROUPS), axis=0, keepdims=True)
        hit = iog == gi
        gsel = jnp.where(hit, 1.0, gsel)
        gscore = jnp.where(hit, neg, gscore)
    val = jnp.where(gsel > 0.5, sel3, neg).reshape(n_exp, tt)
    ioe = lax.broadcasted_iota(I32, (n_exp, tt), 0)
    member = jnp.zeros((n_exp, tt), F32)
    idxs, ws = [], []
    for _ in range(TOP_K):
        m = jnp.max(val, axis=0, keepdims=True)
        ei = jnp.min(jnp.where(val == m, ioe, n_exp), axis=0, keepdims=True)
        hit = ioe == ei
        idxs.append(ei)
        ws.append(jnp.sum(jnp.where(hit, scores, 0.0), axis=0, keepdims=True))
        member = jnp.where(hit, 1.0, member)
        val = jnp.where(hit, neg, val)
    w = jnp.concatenate(ws, axis=0)
    w_ref[...] = w / jnp.sum(w, axis=0, keepdims=True) * ROUTED_SCALE
    idx_ref[...] = jnp.concatenate(idxs, axis=0)
    before = (lax.broadcasted_iota(I32, (tt, tt), 0) < lax.broadcasted_iota(I32, (tt, tt), 1))
    excl = jnp.dot(member.astype(BF16), before.astype(BF16), preferred_element_type=F32) + carry[...]
    ranks = [jnp.sum(jnp.where(ioe == ei, excl, 0.0), axis=0, keepdims=True) for ei in idxs]
    rank_ref[...] = jnp.concatenate(ranks, axis=0).astype(I32)
    carry[...] = carry[...] + jnp.sum(member, axis=1, keepdims=True)
    cnt_ref[...] = carry[...]


def _route_call(logits_t, bias_col):
    n_exp, t = logits_t.shape
    tt = ROUTE_COLS
    return pl.pallas_call(
        _route_kernel,
        grid=(t // tt,),
        in_specs=[pl.BlockSpec((n_exp, tt), lambda i: (0, i)),
                  pl.BlockSpec((n_exp, 1), lambda i: (0, 0))],
        out_specs=[pl.BlockSpec((TOP_K, tt), lambda i: (0, i)),
                   pl.BlockSpec((TOP_K, tt), lambda i: (0, i)),
                   pl.BlockSpec((TOP_K, tt), lambda i: (0, i)),
                   pl.BlockSpec((n_exp, 1), lambda i: (0, 0))],
        out_shape=[jax.ShapeDtypeStruct((TOP_K, t), I32),
                   jax.ShapeDtypeStruct((TOP_K, t), F32),
                   jax.ShapeDtypeStruct((TOP_K, t), I32),
                   jax.ShapeDtypeStruct((n_exp, 1), F32)],
        scratch_shapes=[pltpu.VMEM((n_exp, 1), F32)],
        compiler_params=pltpu.CompilerParams(
            dimension_semantics=("arbitrary",), vmem_limit_bytes=VMEM_LIMIT_OTHER),
        name="route",
    )(logits_t, bias_col)


def _slots_kernel(idx_ref, rank_ref, start_ref, dest_ref):
    n_exp = start_ref.shape[0]
    tt = idx_ref.shape[1]
    ioe = lax.broadcasted_iota(I32, (n_exp, tt), 0)
    start = start_ref[...]
    rows = [jnp.sum(jnp.where(ioe == idx_ref[k:k + 1, :], start, 0.0), axis=0, keepdims=True)
            for k in range(TOP_K)]
    dest_ref[...] = jnp.concatenate(rows, axis=0).astype(I32) + rank_ref[...]


def _slots_call(idx_t, rank_t, start_col):
    k, t = idx_t.shape
    tt = ROUTE_COLS
    return pl.pallas_call(
        _slots_kernel,
        grid=(t // tt,),
        in_specs=[pl.BlockSpec((k, tt), lambda i: (0, i)),
                  pl.BlockSpec((k, tt), lambda i: (0, i)),
                  pl.BlockSpec(start_col.shape, lambda i: (0, 0))],
        out_specs=pl.BlockSpec((k, tt), lambda i: (0, i)),
        out_shape=jax.ShapeDtypeStruct((k, t), I32),
        compiler_params=pltpu.CompilerParams(
            dimension_semantics=("arbitrary",), vmem_limit_bytes=VMEM_LIMIT_OTHER),
        name="slots",
    )(idx_t, rank_t, start_col)


def _dispatch_kernel(zflag_ref, dest_ref, x1_ref, xs_hbm, zbuf, zsem, sem):
    tk = x1_ref.shape[0]
    n_blocks = zflag_ref.shape[0]

    def zero_copy(blk):
        return pltpu.make_async_copy(
            zbuf, xs_hbm.at[pl.ds(pl.multiple_of(blk * EXPERT_BLOCK, EXPERT_BLOCK), EXPERT_BLOCK)],
            zsem)

    @pl.when(pl.program_id(0) == 0)
    def _():
        zbuf[...] = jnp.zeros_like(zbuf)

        def start(blk, c):
            @pl.when(zflag_ref[blk] == 1)
            def _():
                zero_copy(blk).start()
            return c

        def wait(blk, c):
            @pl.when(zflag_ref[blk] == 1)
            def _():
                zero_copy(blk).wait()
            return c

        lax.fori_loop(0, n_blocks, start, 0)
        lax.fori_loop(0, n_blocks, wait, 0)

    def issue(t, c):
        for k in range(TOP_K):
            pltpu.make_async_copy(x1_ref.at[pl.ds(t, 1)], xs_hbm.at[pl.ds(dest_ref[k, t], 1)],
                                  sem).start()
        return c

    lax.fori_loop(0, tk, issue, 0)
    for _ in range(TOP_K):
        pltpu.make_async_copy(x1_ref, xs_hbm.at[pl.ds(0, tk)], sem).wait()


def _dispatch_call(zflag, dest, x1, n_pad):
    t, d = x1.shape
    tk = MOVE_ROWS
    grid_spec = pltpu.PrefetchScalarGridSpec(
        num_scalar_prefetch=1,
        grid=(t // tk,),
        in_specs=[pl.BlockSpec((TOP_K, tk), lambda i, zf: (0, i), memory_space=pltpu.SMEM),
                  pl.BlockSpec((tk, d), lambda i, zf: (i, 0))],
        out_specs=pl.BlockSpec(memory_space=pl.ANY),
        scratch_shapes=[pltpu.VMEM((EXPERT_BLOCK, d), F32),
                        pltpu.SemaphoreType.DMA(()),
                        pltpu.SemaphoreType.DMA(())],
    )
    return pl.pallas_call(
        _dispatch_kernel,
        grid_spec=grid_spec,
        out_shape=jax.ShapeDtypeStruct((n_pad, d), F32),
        compiler_params=pltpu.CompilerParams(
            dimension_semantics=("arbitrary",), vmem_limit_bytes=VMEM_LIMIT_OTHER),
        name="dispatch",
    )(zflag, dest, x1)


def _expert_kernel(be_ref, nu_ref, xs_ref, wg_ref, wu_ref, wd_ref, ys_ref, wgu_s, wd_s):
    i = pl.program_id(0)
    hid = wg_ref.shape[2]
    e = be_ref[i]
    prev = be_ref[jnp.maximum(i - 1, 0)]

    @pl.when((i == 0) | (e != prev))
    def _():
        wgu_s[:, 0:hid] = wg_ref[0].astype(BF16)
        wgu_s[:, hid:2 * hid] = wu_ref[0].astype(BF16)
        wd_s[...] = wd_ref[0].astype(BF16)

    @pl.when(i < nu_ref[0])
    def _():
        gu = jnp.dot(xs_ref[...].astype(BF16), wgu_s[...], preferred_element_type=F32)
        h = _silu(gu[:, :hid]) * gu[:, hid:]
        ys_ref[...] = jnp.dot(h.astype(BF16), wd_s[...], preferred_element_type=F32)

    @pl.when(i >= nu_ref[0])
    def _():
        ys_ref[...] = jnp.zeros_like(ys_ref)


def _expert_call(block_e, n_used, xs, w_gate_e, w_up_e, w_down_e):
    n_pad, d = xs.shape
    n_blocks = n_pad // EXPERT_BLOCK
    hid = w_gate_e.shape[2]

    def xs_map(i, be, nu):
        return (jnp.minimum(i, jnp.maximum(nu[0] - 1, 0)), 0)

    grid_spec = pltpu.PrefetchScalarGridSpec(
        num_scalar_prefetch=2,
        grid=(n_blocks,),
        in_specs=[pl.BlockSpec((EXPERT_BLOCK, d), xs_map),
                  pl.BlockSpec((1, d, hid), lambda i, be, nu: (be[i], 0, 0)),
                  pl.BlockSpec((1, d, hid), lambda i, be, nu: (be[i], 0, 0)),
                  pl.BlockSpec((1, hid, d), lambda i, be, nu: (be[i], 0, 0))],
        out_specs=pl.BlockSpec((EXPERT_BLOCK, d), lambda i, be, nu: (i, 0)),
        scratch_shapes=[pltpu.VMEM((d, 2 * hid), BF16), pltpu.VMEM((hid, d), BF16)],
    )
    return pl.pallas_call(
        _expert_kernel,
        grid_spec=grid_spec,
        out_shape=jax.ShapeDtypeStruct((n_pad, d), F32),
        compiler_params=pltpu.CompilerParams(
            dimension_semantics=("arbitrary",), vmem_limit_bytes=VMEM_LIMIT_OTHER),
        name="experts",
    )(block_e, n_used, xs, w_gate_e, w_up_e, w_down_e)


def _combine_kernel(dest_ref, x1_ref, w_ref, ys_hbm, wsgu_ref, wsd_ref, g2_ref, b2_ref, out_ref,
                    buf, sem, *, alpha):
    tk = x1_ref.shape[0]
    hid = wsd_ref.shape[0]

    def issue(t, c):
        for k in range(TOP_K):
            pltpu.make_async_copy(ys_hbm.at[pl.ds(dest_ref[k, t], 1)], buf.at[k, pl.ds(t, 1)],
                                  sem).start()
        return c

    lax.fori_loop(0, tk, issue, 0)
    x1 = x1_ref[...]
    gu = jnp.dot(x1.astype(BF16), wsgu_ref[...], preferred_element_type=F32)
    hs = _silu(gu[:, :hid]) * gu[:, hid:]
    f = jnp.dot(hs.astype(BF16), wsd_ref[...], preferred_element_type=F32)
    for k in range(TOP_K):
        pltpu.make_async_copy(ys_hbm.at[pl.ds(0, tk)], buf.at[k], sem).wait()
    w = w_ref[...]
    for k in range(TOP_K):
        f = f + w[:, k:k + 1] * buf[k]
    out_ref[...] = _layernorm(alpha * x1 + f, g2_ref[...], b2_ref[...])


def _combine_call(dest, x1, w_tok, ys, ws_gu, ws_d, ln2_g, ln2_b, *, alpha):
    t, d = x1.shape
    tk = MOVE_ROWS

    def const(shape):
        return pl.BlockSpec(shape, lambda i: (0,) * len(shape))

    return pl.pallas_call(
        functools.partial(_combine_kernel, alpha=alpha),
        grid=(t // tk,),
        in_specs=[pl.BlockSpec((TOP_K, tk), lambda i: (0, i), memory_space=pltpu.SMEM),
                  pl.BlockSpec((tk, d), lambda i: (i, 0)),
                  pl.BlockSpec((tk, TOP_K), lambda i: (i, 0)),
                  pl.BlockSpec(memory_space=pl.ANY),
                  const(ws_gu.shape), const(ws_d.shape), const(ln2_g.shape), const(ln2_b.shape)],
        out_specs=pl.BlockSpec((tk, d), lambda i: (i, 0)),
        out_shape=jax.ShapeDtypeStruct((t, d), F32),
        scratch_shapes=[pltpu.VMEM((TOP_K, tk, d), F32), pltpu.SemaphoreType.DMA(())],
        compiler_params=pltpu.CompilerParams(
            dimension_semantics=("arbitrary",), vmem_limit_bytes=VMEM_LIMIT_OTHER),
        name="combine",
    )(dest, x1, w_tok, ys, ws_gu, ws_d, ln2_g, ln2_b)


def _layer(x, w_in, pool_w, pool_scale, conv_dw, conv_ln_g, conv_ln_b, conv_w_out, w_out,
           ln1_g, ln1_b, w_router, router_bias, w_gate_e, w_up_e, w_down_e,
           ws_gate, ws_up, ws_down, ln2_g, ln2_b, *, alpha):
    b, s, d = x.shape
    t = b * s
    row = lambda v: v.reshape(1, -1)
    x1, logits_t = _mixer_call(
        x, w_in.astype(BF16), pool_w.astype(BF16), row(pool_scale), conv_dw, row(conv_ln_g),
        row(conv_ln_b), conv_w_out.astype(BF16), w_out.astype(BF16), row(ln1_g), row(ln1_b),
        w_router.T.astype(BF16), alpha=alpha)
    idx_t, w_t, rank_t, counts = _route_call(logits_t, router_bias.reshape(-1, 1))

    n = t * TOP_K
    n_blocks = -(-(n + N_EXPERTS * (EXPERT_BLOCK - 1)) // EXPERT_BLOCK)
    n_pad = n_blocks * EXPERT_BLOCK
    counts = counts[:, 0].astype(I32)
    padded = ((counts + EXPERT_BLOCK - 1) // EXPERT_BLOCK) * EXPERT_BLOCK
    pad_end = jnp.cumsum(padded)
    pad_start = pad_end - padded
    n_used = (pad_end[-1] // EXPERT_BLOCK).astype(I32)
    blk = jnp.arange(n_blocks, dtype=I32)
    block_e = jnp.minimum(
        jnp.sum((pad_end[None, :] <= (blk * EXPERT_BLOCK)[:, None]).astype(I32), axis=1),
        N_EXPERTS - 1)
    block_e = jnp.where(blk < n_used, block_e, block_e[jnp.maximum(n_used - 1, 0)])
    nxt = jnp.concatenate([block_e[1:], jnp.full((1,), -1, I32)])
    zflag = ((blk >= n_used - 1) | (nxt != block_e)).astype(I32)
    dest = _slots_call(idx_t, rank_t, pad_start.astype(F32).reshape(-1, 1))

    xs = _dispatch_call(zflag, dest, x1, n_pad)
    ys = _expert_call(block_e, n_used.reshape(1), xs, w_gate_e, w_up_e, w_down_e)
    ws_gu = jnp.concatenate([ws_gate, ws_up], axis=1).astype(BF16)
    out = _combine_call(dest, x1, w_t.T, ys, ws_gu, ws_down.astype(BF16), row(ln2_g), row(ln2_b),
                        alpha=alpha)
    return out.reshape(b, s, d)


def kernel(x, w_in, pool_w, pool_scale, conv_dw, conv_ln_g, conv_ln_b, conv_w_out, w_out, ln1_g, ln1_b, w_router, router_bias, w_gate_e, w_up_e, w_down_e, ws_gate, ws_up, ws_down, ln2_g, ln2_b):
    depth = w_in.shape[0]
    alpha = (2.0 * depth) ** 0.25
    for l in range(depth):
        x = _layer(x, w_in[l], pool_w[l], pool_scale[l], conv_dw[l], conv_ln_g[l], conv_ln_b[l],
                   conv_w_out[l], w_out[l], ln1_g[l], ln1_b[l], w_router[l], router_bias[l],
                   w_gate_e[l], w_up_e[l], w_down_e[l], ws_gate[l], ws_up[l], ws_down[l],
                   ln2_g[l], ln2_b[l], alpha=alpha)
    return x
```

```python
import functools

import jax
import jax.numpy as jnp
from jax import lax
from jax.experimental import pallas as pl
from jax.experimental.pallas import tpu as pltpu

F32 = jnp.float32
BF16 = jnp.bfloat16
I32 = jnp.int32
U32 = jnp.uint32

POOL_GROUPS = 4
POOL_WINDOWS = (2, 4, 8, 16)
CONV_KERNEL = 31
N_EXPERTS = 256
TOP_K = 8
N_GROUPS = 8
TOPK_GROUPS = 4
EXPERTS_PER_GROUP = N_EXPERTS // N_GROUPS
ROUTED_SCALE = 2.5
LN_EPS = 1e-5

MIX_ROWS = 512
MIX_CHUNK = 32
POOL_HIST = 16
CONV_HIST = 32
TILE = 256
LANES = 128
RUN_ROWS = 16
MAX_CHUNKS = N_EXPERTS + TILE * TOP_K // RUN_ROWS
BUF_ROWS = MAX_CHUNKS * RUN_ROWS
SORT_BLOCK = 512
EXPERT_BLOCK = 256
VMEM_LIMIT_MIXER = 56 * 1024 * 1024
VMEM_LIMIT_OTHER = 48 * 1024 * 1024


def _layernorm(z, g, b):
    mu = jnp.mean(z, axis=-1, keepdims=True)
    d = z - mu
    var = jnp.mean(d * d, axis=-1, keepdims=True)
    return d * lax.rsqrt(var + LN_EPS) * g + b


def _silu(v):
    return v * jax.nn.sigmoid(v)


def _store_packed_rows(ref, row0, v):
    n, d = v.shape
    half = d // 2
    sub = half // LANES
    for j in range(sub):
        words = pltpu.pack_elementwise(
            [v[:, j * LANES:(j + 1) * LANES], v[:, half + j * LANES:half + (j + 1) * LANES]],
            packed_dtype=BF16)
        ref[pl.ds(row0 * sub + j, n, stride=sub), :] = words


def _load_packed_rows(ref, row0, n, sub):
    los, his = [], []
    for j in range(sub):
        words = ref[pl.ds(row0 * sub + j, n, stride=sub), :]
        los.append(pltpu.unpack_elementwise(words, index=0, packed_dtype=BF16,
                                            unpacked_dtype=F32).astype(BF16))
        his.append(pltpu.unpack_elementwise(words, index=1, packed_dtype=BF16,
                                            unpacked_dtype=F32).astype(BF16))
    return jnp.concatenate(los + his, axis=1)


def _mixer_kernel(x_ref, w_in_ref, pool_w_ref, pool_scale_ref, dw_ref, cg_ref, cb_ref, cwo_ref,
                  w_out_ref, g1_ref, b1_ref, wr_ref, x1_ref, lt_ref,
                  ubuf, vbuf, rbuf, cbuf, *, alpha):
    ts = x_ref.shape[1]
    d_model = x_ref.shape[2]
    pw = ubuf.shape[1]
    cw = vbuf.shape[1]
    gi = pw // POOL_GROUPS
    si = pl.program_id(1)

    @pl.when(si == 0)
    def _():
        ubuf[0:POOL_HIST, :] = jnp.zeros((POOL_HIST, pw), F32)
        vbuf[0:CONV_HIST, :] = jnp.zeros((CONV_HIST, cw), F32)

    x = x_ref[0]
    xb = x.astype(BF16)
    ubuf[POOL_HIST:POOL_HIST + ts, :] = jnp.dot(xb, w_in_ref[:, 0:pw], preferred_element_type=F32)
    a = jnp.dot(xb, w_in_ref[:, pw:pw + 2 * cw], preferred_element_type=F32)
    vbuf[CONV_HIST:CONV_HIST + ts, :] = a[:, :cw] * jax.nn.sigmoid(a[:, cw:])
    pos0 = si * ts

    for c in range(ts // MIX_CHUNK):
        r0 = c * MIX_CHUNK
        pos = pos0 + r0 + lax.broadcasted_iota(I32, (MIX_CHUNK, 1), 0)
        for g, w in enumerate(POOL_WINDOWS):
            cols = slice(g * gi, (g + 1) * gi)
            cur = ubuf[pl.ds(POOL_HIST + r0, MIX_CHUNK), cols]
            s = cur
            for j in range(1, w):
                s = s + ubuf[pl.ds(POOL_HIST + r0 - j, MIX_CHUNK), cols]
            cnt = jnp.minimum(pos + 1, w).astype(F32)
            rbuf[pl.ds(r0, MIX_CHUNK), cols] = (s / cnt - cur).astype(BF16)
        base = CONV_HIST + r0 - (CONV_KERNEL - 1)
        acc = dw_ref[0:1, :] * vbuf[pl.ds(base, MIX_CHUNK), :]
        for k in range(1, CONV_KERNEL):
            acc = acc + dw_ref[k:k + 1, :] * vbuf[pl.ds(base + k, MIX_CHUNK), :]
        n = _layernorm(acc, cg_ref[...], cb_ref[...])
        cbuf[pl.ds(r0, MIX_CHUNK), :] = _silu(n).astype(BF16)

    ubuf[0:POOL_HIST, :] = ubuf[ts:ts + POOL_HIST, :]
    vbuf[0:CONV_HIST, :] = vbuf[ts:ts + CONV_HIST, :]

    y_pool = jnp.concatenate(
        [jnp.dot(rbuf[:, g * gi:(g + 1) * gi], pool_w_ref[g], preferred_element_type=F32)
         for g in range(POOL_GROUPS)], axis=1) * pool_scale_ref[...]
    y_conv = jnp.dot(cbuf[...], cwo_ref[...], preferred_element_type=F32)
    gates = jnp.dot(xb, w_in_ref[:, pw + 2 * cw:], preferred_element_type=F32)
    merged = (jax.nn.sigmoid(gates[:, :d_model]) * y_pool
              + jax.nn.sigmoid(gates[:, d_model:]) * y_conv)
    m = jnp.dot(merged.astype(BF16), w_out_ref[...], preferred_element_type=F32)
    x1 = _layernorm(alpha * x + m, g1_ref[...], b1_ref[...])
    x1_ref[...] = x1
    lt_ref[...] = lax.dot_general(wr_ref[...], x1.astype(BF16), (((1,), (1,)), ((), ())),
                                  preferred_element_type=F32)


def _mixer_call(x, w_in, pool_w, pool_scale, conv_dw, cln_g, cln_b, conv_w_out, w_out,
                ln1_g, ln1_b, w_router_t, *, alpha):
    b, s, d = x.shape
    ts = MIX_ROWS
    ns = s // ts
    pw = pool_w.shape[0] * pool_w.shape[1]
    cw = conv_dw.shape[1]
    n_exp = w_router_t.shape[0]

    def const(shape):
        return pl.BlockSpec(shape, lambda bi, si: (0,) * len(shape))

    return pl.pallas_call(
        functools.partial(_mixer_kernel, alpha=alpha),
        grid=(b, ns),
        in_specs=[
            pl.BlockSpec((1, ts, d), lambda bi, si: (bi, si, 0)),
            const(w_in.shape), const(pool_w.shape), const(pool_scale.shape), const(conv_dw.shape),
            const(cln_g.shape), const(cln_b.shape), const(conv_w_out.shape), const(w_out.shape),
            const(ln1_g.shape), const(ln1_b.shape), const(w_router_t.shape),
        ],
        out_specs=[
            pl.BlockSpec((ts, d), lambda bi, si: (bi * ns + si, 0)),
            pl.BlockSpec((n_exp, ts), lambda bi, si: (0, bi * ns + si)),
        ],
        out_shape=[jax.ShapeDtypeStruct((b * s, d), F32),
                   jax.ShapeDtypeStruct((n_exp, b * s), F32)],
        scratch_shapes=[
            pltpu.VMEM((POOL_HIST + ts, pw), F32),
            pltpu.VMEM((CONV_HIST + ts, cw), F32),
            pltpu.VMEM((ts, pw), BF16),
            pltpu.VMEM((ts, cw), BF16),
        ],
        compiler_params=pltpu.CompilerParams(
            dimension_semantics=("arbitrary", "arbitrary"), vmem_limit_bytes=VMEM_LIMIT_MIXER),
        name="mixer",
    )(x, w_in, pool_w, pool_scale, conv_dw, cln_g, cln_b, conv_w_out, w_out, ln1_g, ln1_b,
      w_router_t)


def _route_kernel(lt_ref, bias_ref, idx_ref, w_ref, bpos_ref, tab_ref, cnt_ref, carry):
    n_exp, tt = lt_ref.shape
    neg = -jnp.inf

    @pl.when(pl.program_id(0) == 0)
    def _():
        carry[...] = jnp.zeros_like(carry)

    scores = jax.nn.sigmoid(lt_ref[...])
    sel = scores + bias_ref[...]
    sel3 = sel.reshape(N_GROUPS, EXPERTS_PER_GROUP, tt)
    io3 = lax.broadcasted_iota(I32, sel3.shape, 1)
    m1 = jnp.max(sel3, axis=1, keepdims=True)
    i1 = jnp.min(jnp.where(sel3 == m1, io3, EXPERTS_PER_GROUP), axis=1, keepdims=True)
    m2 = jnp.max(jnp.where(io3 == i1, neg, sel3), axis=1, keepdims=True)
    gscore = m1 + m2
    iog = lax.broadcasted_iota(I32, gscore.shape, 0)
    gsel = jnp.zeros(gscore.shape, F32)
    for _ in range(TOPK_GROUPS):
        m = jnp.max(gscore, axis=0, keepdims=True)
        gi = jnp.min(jnp.where(gscore == m, iog, N_GROUPS), axis=0, keepdims=True)
        hit = iog == gi
        gsel = jnp.where(hit, 1.0, gsel)
        gscore = jnp.where(hit, neg, gscore)
    val = jnp.where(gsel > 0.5, sel3, neg).reshape(n_exp, tt)
    ioe = lax.broadcasted_iota(I32, (n_exp, tt), 0)
    member = jnp.zeros((n_exp, tt), F32)
    idxs, ws = [], []
    for _ in range(TOP_K):
        m = jnp.max(val, axis=0, keepdims=True)
        ei = jnp.min(jnp.where(val == m, ioe, n_exp), axis=0, keepdims=True)
        hit = ioe == ei
        idxs.append(ei)
        ws.append(jnp.sum(jnp.where(hit, scores, 0.0), axis=0, keepdims=True))
        member = jnp.where(hit, 1.0, member)
        val = jnp.where(hit, neg, val)
    w = jnp.concatenate(ws, axis=0)
    w_ref[...] = w / jnp.sum(w, axis=0, keepdims=True) * ROUTED_SCALE
    idx_ref[...] = jnp.concatenate(idxs, axis=0)

    mb = member.astype(BF16)
    before = (lax.broadcasted_iota(I32, (tt, tt), 0) < lax.broadcasted_iota(I32, (tt, tt), 1))
    rank_in_tile = jnp.dot(mb, before.astype(BF16), preferred_element_type=F32)
    inv_r = 1.0 / RUN_ROWS
    c_col = jnp.sum(member, axis=1, keepdims=True)
    nch_col = jnp.maximum(jnp.floor((c_col + (RUN_ROWS - 1)) * inv_r), 1.0)
    c_row = lax.dot_general(jnp.ones((8, tt), BF16), mb, (((1,), (1,)), ((), ())),
                            preferred_element_type=F32)
    nch_row = jnp.maximum(jnp.floor((c_row + (RUN_ROWS - 1)) * inv_r), 1.0)
    ee0 = lax.broadcasted_iota(I32, (n_exp, n_exp), 0)
    ee1 = lax.broadcasted_iota(I32, (n_exp, n_exp), 1)
    cs_col = jnp.dot((ee1 < ee0).astype(BF16),
                     jnp.broadcast_to(nch_col, (n_exp, 128)).astype(BF16),
                     preferred_element_type=F32)[:, 0:1]
    cs_row = jnp.dot(nch_row.astype(BF16), (ee0 < ee1).astype(BF16),
                     preferred_element_type=F32)
    pos_full = cs_col * RUN_ROWS + rank_in_tile
    bpos = [jnp.sum(jnp.where(ioe == ei, pos_full, 0.0), axis=0, keepdims=True) for ei in idxs]
    bpos_ref[...] = jnp.concatenate(bpos, axis=0).astype(I32)
    tab_ref[0] = jnp.concatenate([carry[0:1], nch_row[0:1], cs_row[0:1], c_row[0:1],
                                  jnp.zeros((4, n_exp), F32)], axis=0)
    carry[...] = carry[...] + (c_row + (c_row - 2.0 * jnp.floor(c_row * 0.5)))
    cnt_ref[...] = carry[...]


def _route_call(logits_t, bias_col):
    n_exp, t = logits_t.shape
    tt = TILE
    return pl.pallas_call(
        _route_kernel,
        grid=(t // tt,),
        in_specs=[pl.BlockSpec((n_exp, tt), lambda i: (0, i)),
                  pl.BlockSpec((n_exp, 1), lambda i: (0, 0))],
        out_specs=[pl.BlockSpec((TOP_K, tt), lambda i: (0, i)),
                   pl.BlockSpec((TOP_K, tt), lambda i: (0, i)),
                   pl.BlockSpec((TOP_K, tt), lambda i: (0, i)),
                   pl.BlockSpec((1, 8, n_exp), lambda i: (i, 0, 0)),
                   pl.BlockSpec((8, n_exp), lambda i: (0, 0))],
        out_shape=[jax.ShapeDtypeStruct((TOP_K, t), I32),
                   jax.ShapeDtypeStruct((TOP_K, t), F32),
                   jax.ShapeDtypeStruct((TOP_K, t), I32),
                   jax.ShapeDtypeStruct((t // tt, 8, n_exp), F32),
                   jax.ShapeDtypeStruct((8, n_exp), F32)],
        scratch_shapes=[pltpu.VMEM((8, n_exp), F32)],
        compiler_params=pltpu.CompilerParams(
            dimension_semantics=("arbitrary",), vmem_limit_bytes=VMEM_LIMIT_OTHER),
        name="route",
    )(logits_t, bias_col)


TAB_SLOT = 0
TAB_NCH = 1
TAB_CHUNK = 2
TAB_META = 3


def _chunk_loops(tab_ref, n_exp, make_copy):
    def first(e, c):
        make_copy(tab_ref[0, TAB_CHUNK, e], tab_ref[0, TAB_SLOT, e]).start()
        return c

    lax.fori_loop(0, n_exp, first, 0, unroll=8)

    @pl.when(tab_ref[0, TAB_META, 1] > 1)
    def _():
        def extra(e, c):
            def one(i, c2):
                make_copy(tab_ref[0, TAB_CHUNK, e] + i,
                          tab_ref[0, TAB_SLOT, e] + i * RUN_ROWS).start()
                return c2
            lax.fori_loop(1, tab_ref[0, TAB_NCH, e], one, 0)
            return c
        lax.fori_loop(0, n_exp, extra, 0)


def _wait_chunks(tab_ref, make_copy):
    def wait(i, c):
        make_copy(0, 0).wait()
        return c
    lax.fori_loop(0, tab_ref[0, TAB_META, 0], wait, 0)


def _dispatch_kernel(zflag_ref, tab_ref, bpos_ref, x1_ref, xs_hbm, sbuf, zbuf, zsem, sem):
    tt = x1_ref.shape[0]
    n_blocks = zflag_ref.shape[0]
    n_exp = tab_ref.shape[2]
    sub = zbuf.shape[0] // EXPERT_BLOCK
    chunk_rows = RUN_ROWS * sub

    def zero_copy(blk):
        return pltpu.make_async_copy(
            zbuf, xs_hbm.at[pl.ds(pl.multiple_of(blk * zbuf.shape[0], zbuf.shape[0]),
                                  zbuf.shape[0])], zsem)

    @pl.when(pl.program_id(0) == 0)
    def _():
        zbuf[...] = jnp.zeros_like(zbuf)

        def start(blk, c):
            @pl.when(zflag_ref[blk] == 1)
            def _():
                zero_copy(blk).start()
            return c

        def wait(blk, c):
            @pl.when(zflag_ref[blk] == 1)
            def _():
                zero_copy(blk).wait()
            return c

        lax.fori_loop(0, n_blocks, start, 0)
        lax.fori_loop(0, n_blocks, wait, 0)

    n_rows = tab_ref[0, TAB_META, 0] * RUN_ROWS
    xb = x1_ref[...].astype(BF16)
    for rb in range(BUF_ROWS // SORT_BLOCK):
        @pl.when(rb * SORT_BLOCK < n_rows)
        def _():
            ior = lax.broadcasted_iota(I32, (SORT_BLOCK, tt), 0) + rb * SORT_BLOCK
            p = jnp.zeros((SORT_BLOCK, tt), F32)
            for k in range(TOP_K):
                p = jnp.where(ior == bpos_ref[k:k + 1, :], 1.0, p)
            srt = jnp.dot(p.astype(BF16), xb, preferred_element_type=F32)
            _store_packed_rows(sbuf, rb * SORT_BLOCK, srt)

    def make_copy(chunk, slot):
        return pltpu.make_async_copy(
            sbuf.at[pl.ds(pl.multiple_of(chunk * chunk_rows, chunk_rows), chunk_rows)],
            xs_hbm.at[pl.ds(pl.multiple_of(slot * sub, 2 * sub), chunk_rows)], sem)

    _chunk_loops(tab_ref, n_exp, make_copy)
    _wait_chunks(tab_ref, make_copy)


def _dispatch_call(zflag, tab, bpos, x1, n_slots):
    t, d = x1.shape
    tt = TILE
    n_exp = tab.shape[2]
    sub = d // 2 // LANES
    grid_spec = pltpu.PrefetchScalarGridSpec(
        num_scalar_prefetch=1,
        grid=(t // tt,),
        in_specs=[pl.BlockSpec((1, 4, n_exp), lambda i, zf: (i, 0, 0), memory_space=pltpu.SMEM),
                  pl.BlockSpec((TOP_K, tt), lambda i, zf: (0, i)),
                  pl.BlockSpec((tt, d), lambda i, zf: (i, 0))],
        out_specs=pl.BlockSpec(memory_space=pl.ANY),
        scratch_shapes=[pltpu.VMEM((BUF_ROWS * sub, LANES), U32),
                        pltpu.VMEM((EXPERT_BLOCK * sub, LANES), U32),
                        pltpu.SemaphoreType.DMA(()),
                        pltpu.SemaphoreType.DMA(())],
    )
    return pl.pallas_call(
        _dispatch_kernel,
        grid_spec=grid_spec,
        out_shape=jax.ShapeDtypeStruct((n_slots * sub, LANES), U32),
        compiler_params=pltpu.CompilerParams(
            dimension_semantics=("arbitrary",), vmem_limit_bytes=VMEM_LIMIT_OTHER),
        name="dispatch",
    )(zflag, tab, bpos, x1)


def _expert_kernel(be_ref, nu_ref, xs_ref, wg_ref, wu_ref, wd_ref, ys_ref, wgu_s, wd_s):
    i = pl.program_id(0)
    hid = wg_ref.shape[2]
    sub = xs_ref.shape[0] // EXPERT_BLOCK
    e = be_ref[i]
    prev = be_ref[jnp.maximum(i - 1, 0)]

    @pl.when((i == 0) | (e != prev))
    def _():
        wgu_s[:, 0:hid] = wg_ref[0].astype(BF16)
        wgu_s[:, hid:2 * hid] = wu_ref[0].astype(BF16)
        wd_s[...] = wd_ref[0].astype(BF16)

    @pl.when(i < nu_ref[0])
    def _():
        xb = _load_packed_rows(xs_ref, 0, EXPERT_BLOCK, sub)
        gu = jnp.dot(xb, wgu_s[...], preferred_element_type=F32)
        h = _silu(gu[:, :hid]) * gu[:, hid:]
        _store_packed_rows(ys_ref, 0,
                           jnp.dot(h.astype(BF16), wd_s[...], preferred_element_type=F32))

    @pl.when(i >= nu_ref[0])
    def _():
        ys_ref[...] = jnp.zeros_like(ys_ref)


def _expert_call(block_e, n_used, xs, w_gate_e, w_up_e, w_down_e):
    d = w_gate_e.shape[1]
    hid = w_gate_e.shape[2]
    blk_rows = EXPERT_BLOCK * (d // 2 // LANES)
    n_blocks = xs.shape[0] // blk_rows

    def xs_map(i, be, nu):
        return (jnp.minimum(i, jnp.maximum(nu[0] - 1, 0)), 0)

    grid_spec = pltpu.PrefetchScalarGridSpec(
        num_scalar_prefetch=2,
        grid=(n_blocks,),
        in_specs=[pl.BlockSpec((blk_rows, LANES), xs_map),
                  pl.BlockSpec((1, d, hid), lambda i, be, nu: (be[i], 0, 0)),
                  pl.BlockSpec((1, d, hid), lambda i, be, nu: (be[i], 0, 0)),
                  pl.BlockSpec((1, hid, d), lambda i, be, nu: (be[i], 0, 0))],
        out_specs=pl.BlockSpec((blk_rows, LANES), lambda i, be, nu: (i, 0)),
        scratch_shapes=[pltpu.VMEM((d, 2 * hid), BF16), pltpu.VMEM((hid, d), BF16)],
    )
    return pl.pallas_call(
        _expert_kernel,
        grid_spec=grid_spec,
        out_shape=jax.ShapeDtypeStruct(xs.shape, U32),
        compiler_params=pltpu.CompilerParams(
            dimension_semantics=("arbitrary",), vmem_limit_bytes=VMEM_LIMIT_OTHER),
        name="experts",
    )(block_e, n_used, xs, w_gate_e, w_up_e, w_down_e)


def _combine_kernel(tab_ref, x1_ref, bpos_ref, w_ref, ys_hbm, wsgu_ref, wsd_ref, g2_ref, b2_ref,
                    out_ref, buf, acc, sem, *, alpha):
    tt = x1_ref.shape[0]
    hid = wsd_ref.shape[0]
    sub = buf.shape[0] // BUF_ROWS
    chunk_rows = RUN_ROWS * sub
    n_exp = tab_ref.shape[2]

    @pl.when(pl.program_id(0) == 0)
    def _():
        buf[...] = jnp.zeros_like(buf)

    def make_copy(chunk, slot):
        return pltpu.make_async_copy(
            ys_hbm.at[pl.ds(pl.multiple_of(slot * sub, 2 * sub), chunk_rows)],
            buf.at[pl.ds(pl.multiple_of(chunk * chunk_rows, chunk_rows), chunk_rows)], sem)

    _chunk_loops(tab_ref, n_exp, make_copy)

    x1 = x1_ref[...]
    gu = jnp.dot(x1.astype(BF16), wsgu_ref[...], preferred_element_type=F32)
    hs = _silu(gu[:, :hid]) * gu[:, hid:]
    acc[...] = jnp.dot(hs.astype(BF16), wsd_ref[...], preferred_element_type=F32)

    _wait_chunks(tab_ref, make_copy)

    n_rows = tab_ref[0, TAB_META, 0] * RUN_ROWS
    bpos = bpos_ref[...]
    w = w_ref[...]
    for kb in range(BUF_ROWS // SORT_BLOCK):
        @pl.when(kb * SORT_BLOCK < n_rows)
        def _():
            iol = lax.broadcasted_iota(I32, (tt, SORT_BLOCK), 1) + kb * SORT_BLOCK
            q = jnp.zeros((tt, SORT_BLOCK), F32)
            for k in range(TOP_K):
                q = jnp.where(iol == bpos[:, k:k + 1], w[:, k:k + 1], q)
            rows = _load_packed_rows(buf, kb * SORT_BLOCK, SORT_BLOCK, sub)
            acc[...] += jnp.dot(q.astype(BF16), rows, preferred_element_type=F32)

    out_ref[...] = _layernorm(alpha * x1 + acc[...], g2_ref[...], b2_ref[...])


def _combine_call(tab, x1, bpos_tok, w_tok, ys, ws_gu, ws_d, ln2_g, ln2_b, *, alpha):
    t, d = x1.shape
    tt = TILE
    n_exp = tab.shape[2]

    def const(shape):
        return pl.BlockSpec(shape, lambda i: (0,) * len(shape))

    return pl.pallas_call(
        functools.partial(_combine_kernel, alpha=alpha),
        grid=(t // tt,),
        in_specs=[pl.BlockSpec((1, 4, n_exp), lambda i: (i, 0, 0), memory_space=pltpu.SMEM),
                  pl.BlockSpec((tt, d), lambda i: (i, 0)),
                  pl.BlockSpec((tt, TOP_K), lambda i: (i, 0)),
                  pl.BlockSpec((tt, TOP_K), lambda i: (i, 0)),
                  pl.BlockSpec(memory_space=pl.ANY),
                  const(ws_gu.shape), const(ws_d.shape), const(ln2_g.shape), const(ln2_b.shape)],
        out_specs=pl.BlockSpec((tt, d), lambda i: (i, 0)),
        out_shape=jax.ShapeDtypeStruct((t, d), F32),
        scratch_shapes=[pltpu.VMEM((BUF_ROWS * (d // 2 // LANES), LANES), U32),
                        pltpu.VMEM((tt, d), F32),
                        pltpu.SemaphoreType.DMA(())],
        compiler_params=pltpu.CompilerParams(
            dimension_semantics=("arbitrary",), vmem_limit_bytes=VMEM_LIMIT_OTHER),
        name="combine",
    )(tab, x1, bpos_tok, w_tok, ys, ws_gu, ws_d, ln2_g, ln2_b)


def _layer(x, w_in, pool_w, pool_scale, conv_dw, conv_ln_g, conv_ln_b, conv_w_out, w_out,
           ln1_g, ln1_b, w_router, router_bias, w_gate_e, w_up_e, w_down_e,
           ws_gate, ws_up, ws_down, ln2_g, ln2_b, *, alpha):
    b, s, d = x.shape
    t = b * s
    row = lambda v: v.reshape(1, -1)
    x1, logits_t = _mixer_call(
        x, w_in.astype(BF16), pool_w.astype(BF16), row(pool_scale), conv_dw, row(conv_ln_g),
        row(conv_ln_b), conv_w_out.astype(BF16), w_out.astype(BF16), row(ln1_g), row(ln1_b),
        w_router.T.astype(BF16), alpha=alpha)
    _, w_t, bpos_t, tab, counts = _route_call(logits_t, router_bias.reshape(-1, 1))

    n = t * TOP_K
    n_blocks = -(-(n + (t // TILE) * N_EXPERTS + N_EXPERTS * (EXPERT_BLOCK - 1 + RUN_ROWS))
                 // EXPERT_BLOCK)
    n_slots = n_blocks * EXPERT_BLOCK
    counts = counts[0].astype(I32)
    padded = ((counts + RUN_ROWS + EXPERT_BLOCK - 1) // EXPERT_BLOCK) * EXPERT_BLOCK
    pad_end = jnp.cumsum(padded)
    pad_start = pad_end - padded
    n_used = (pad_end[-1] // EXPERT_BLOCK).astype(I32)
    blk = jnp.arange(n_blocks, dtype=I32)
    block_e = jnp.minimum(
        jnp.sum((pad_end[None, :] <= (blk * EXPERT_BLOCK)[:, None]).astype(I32), axis=1),
        N_EXPERTS - 1)
    real_end = (pad_start + counts)[block_e]
    zflag = ((blk >= n_used) | ((blk + 1) * EXPERT_BLOCK > real_end)).astype(I32)
    block_e = jnp.where(blk < n_used, block_e, block_e[jnp.maximum(n_used - 1, 0)])

    tab = tab.astype(I32)
    nch = tab[:, 1, :]
    meta = jnp.zeros_like(nch).at[:, 0].set(jnp.sum(nch, axis=1)).at[:, 1].set(jnp.max(nch, axis=1))
    tab = jnp.stack([tab[:, 0, :] + pad_start[None, :], nch, tab[:, 2, :], meta], axis=1)

    xs = _dispatch_call(zflag, tab, bpos_t, x1, n_slots)
    ys = _expert_call(block_e, n_used.reshape(1), xs, w_gate_e, w_up_e, w_down_e)
    ws_gu = jnp.concatenate([ws_gate, ws_up], axis=1).astype(BF16)
    out = _combine_call(tab, x1, bpos_t.T, w_t.T, ys, ws_gu, ws_down.astype(BF16), row(ln2_g),
                        row(ln2_b), alpha=alpha)
    return out.reshape(b, s, d)


def kernel(x, w_in, pool_w, pool_scale, conv_dw, conv_ln_g, conv_ln_b, conv_w_out, w_out, ln1_g, ln1_b, w_router, router_bias, w_gate_e, w_up_e, w_down_e, ws_gate, ws_up, ws_down, ln2_g, ln2_b):
    depth = w_in.shape[0]
    alpha = (2.0 * depth) ** 0.25
    for l in range(depth):
        x = _layer(x, w_in[l], pool_w[l], pool_scale[l], conv_dw[l], conv_ln_g[l], conv_ln_b[l],
                   conv_w_out[l], w_out[l], ln1_g[l], ln1_b[l], w_router[l], router_bias[l],
                   w_gate_e[l], w_up_e[l], w_down_e[l], ws_gate[l], ws_up[l], ws_down[l],
                   ln2_g[l], ln2_b[l], alpha=alpha)
    return x
```

```python
import functools

import jax
import jax.numpy as jnp
from jax import lax
from jax.experimental import pallas as pl
from jax.experimental.pallas import tpu as pltpu

F32 = jnp.float32
BF16 = jnp.bfloat16
I32 = jnp.int32
U32 = jnp.uint32

POOL_GROUPS = 4
POOL_WINDOWS = (2, 4, 8, 16)
CONV_KERNEL = 31
N_EXPERTS = 256
TOP_K = 8
N_GROUPS = 8
TOPK_GROUPS = 4
EXPERTS_PER_GROUP = N_EXPERTS // N_GROUPS
ROUTED_SCALE = 2.5
LN_EPS = 1e-5

SUBLANES = 8
LANES = 128
MIX_ROWS = 512
POOL_CHUNK = 32
CONV_CHUNK = 64
CONV_COLS = 256
POOL_HIST = 16
CONV_HIST = 32
TILE = 256
RUN_ROWS = 16
SORT_BLOCK = 512
MAX_CHUNKS = N_EXPERTS + TILE * TOP_K // RUN_ROWS
BUF_ROWS = MAX_CHUNKS * RUN_ROWS
SORT_ROWS = -(-(TILE * TOP_K + N_EXPERTS + RUN_ROWS) // SORT_BLOCK) * SORT_BLOCK
EXPERT_BLOCK = 256
VMEM_LIMIT_MIXER = 56 * 1024 * 1024
VMEM_LIMIT_OTHER = 48 * 1024 * 1024


def _layernorm(z, g, b):
    mu = jnp.mean(z, axis=-1, keepdims=True)
    d = z - mu
    var = jnp.mean(d * d, axis=-1, keepdims=True)
    return d * lax.rsqrt(var + LN_EPS) * g + b


def _silu(v):
    return v * jax.nn.sigmoid(v)


def _store_packed_rows(ref, row0, v):
    n, d = v.shape
    half = d // 2
    sub = half // LANES
    for j in range(sub):
        words = pltpu.pack_elementwise(
            [v[:, j * LANES:(j + 1) * LANES], v[:, half + j * LANES:half + (j + 1) * LANES]],
            packed_dtype=BF16)
        ref[pl.ds(row0 * sub + j, n, stride=sub), :] = words


def _load_packed_rows(ref, row0, n, sub):
    los, his = [], []
    for j in range(sub):
        words = ref[pl.ds(row0 * sub + j, n, stride=sub), :]
        los.append(pltpu.unpack_elementwise(words, index=0, packed_dtype=BF16,
                                            unpacked_dtype=F32).astype(BF16))
        his.append(pltpu.unpack_elementwise(words, index=1, packed_dtype=BF16,
                                            unpacked_dtype=F32).astype(BF16))
    return jnp.concatenate(los + his, axis=1)


def _mixer_kernel(x_ref, w_in_ref, pool_w_ref, pool_scale_ref, dw_ref, cg_ref, cb_ref, cwo_ref,
                  w_out_ref, g1_ref, b1_ref, wr_ref, x1_ref, lt_ref,
                  ubuf, vbuf, rbuf, cvbuf, *, alpha):
    ts = x_ref.shape[1]
    d_model = x_ref.shape[2]
    pw = ubuf.shape[1]
    cw = vbuf.shape[1]
    gi = pw // POOL_GROUPS
    si = pl.program_id(1)

    @pl.when(si == 0)
    def _():
        ubuf[0:POOL_HIST, :] = jnp.zeros((POOL_HIST, pw), F32)
        vbuf[0:CONV_HIST, :] = jnp.zeros((CONV_HIST, cw), F32)

    x = x_ref[0]
    xb = x.astype(BF16)
    ubuf[POOL_HIST:POOL_HIST + ts, :] = jnp.dot(xb, w_in_ref[:, 0:pw], preferred_element_type=F32)
    a = jnp.dot(xb, w_in_ref[:, pw:pw + 2 * cw], preferred_element_type=F32)
    vbuf[CONV_HIST:CONV_HIST + ts, :] = a[:, :cw] * jax.nn.sigmoid(a[:, cw:])
    pos0 = si * ts

    for c in range(ts // POOL_CHUNK):
        r0 = c * POOL_CHUNK
        pos = pos0 + r0 + lax.broadcasted_iota(I32, (POOL_CHUNK, 1), 0)
        for g, w in enumerate(POOL_WINDOWS):
            cols = slice(g * gi, (g + 1) * gi)
            cur = ubuf[pl.ds(POOL_HIST + r0, POOL_CHUNK), cols]
            s = cur
            for j in range(1, w):
                s = s + ubuf[pl.ds(POOL_HIST + r0 - j, POOL_CHUNK), cols]
            cnt = jnp.minimum(pos + 1, w).astype(F32)
            rbuf[pl.ds(r0, POOL_CHUNK), cols] = (s / cnt - cur).astype(BF16)

    for c in range(ts // CONV_CHUNK):
        r0 = c * CONV_CHUNK
        base = CONV_HIST + r0 - (CONV_KERNEL - 1)
        for c0 in range(0, cw, CONV_COLS):
            cols = slice(c0, c0 + CONV_COLS)
            acc = None
            for rho in range(SUBLANES):
                taps = range(rho, CONV_KERNEL, SUBLANES)
                win = vbuf[pl.ds(base + rho, CONV_CHUNK + taps[-1] - rho), cols]
                for k in taps:
                    term = dw_ref[k:k + 1, cols] * win[k - rho:k - rho + CONV_CHUNK, :]
                    acc = term if acc is None else acc + term
            cvbuf[pl.ds(r0, CONV_CHUNK), cols] = acc

    ubuf[0:POOL_HIST, :] = ubuf[ts:ts + POOL_HIST, :]
    vbuf[0:CONV_HIST, :] = vbuf[ts:ts + CONV_HIST, :]

    y_pool = jnp.concatenate(
        [jnp.dot(rbuf[:, g * gi:(g + 1) * gi], pool_w_ref[g], preferred_element_type=F32)
         for g in range(POOL_GROUPS)], axis=1) * pool_scale_ref[...]
    conv = _silu(_layernorm(cvbuf[...], cg_ref[...], cb_ref[...]))
    y_conv = jnp.dot(conv.astype(BF16), cwo_ref[...], preferred_element_type=F32)
    gates = jnp.dot(xb, w_in_ref[:, pw + 2 * cw:], preferred_element_type=F32)
    merged = (jax.nn.sigmoid(gates[:, :d_model]) * y_pool
              + jax.nn.sigmoid(gates[:, d_model:]) * y_conv)
    m = jnp.dot(merged.astype(BF16), w_out_ref[...], preferred_element_type=F32)
    x1 = _layernorm(alpha * x + m, g1_ref[...], b1_ref[...])
    x1_ref[...] = x1
    lt_ref[...] = lax.dot_general(wr_ref[...], x1.astype(BF16), (((1,), (1,)), ((), ())),
                                  preferred_element_type=F32)


def _mixer_call(x, w_in, pool_w, pool_scale, conv_dw, cln_g, cln_b, conv_w_out, w_out,
                ln1_g, ln1_b, w_router_t, *, alpha):
    b, s, d = x.shape
    ts = MIX_ROWS
    ns = s // ts
    pw = pool_w.shape[0] * pool_w.shape[1]
    cw = conv_dw.shape[1]
    n_exp = w_router_t.shape[0]

    def const(shape):
        return pl.BlockSpec(shape, lambda bi, si: (0,) * len(shape))

    return pl.pallas_call(
        functools.partial(_mixer_kernel, alpha=alpha),
        grid=(b, ns),
        in_specs=[
            pl.BlockSpec((1, ts, d), lambda bi, si: (bi, si, 0)),
            const(w_in.shape), const(pool_w.shape), const(pool_scale.shape), const(conv_dw.shape),
            const(cln_g.shape), const(cln_b.shape), const(conv_w_out.shape), const(w_out.shape),
            const(ln1_g.shape), const(ln1_b.shape), const(w_router_t.shape),
        ],
        out_specs=[
            pl.BlockSpec((ts, d), lambda bi, si: (bi * ns + si, 0)),
            pl.BlockSpec((n_exp, ts), lambda bi, si: (0, bi * ns + si)),
        ],
        out_shape=[jax.ShapeDtypeStruct((b * s, d), F32),
                   jax.ShapeDtypeStruct((n_exp, b * s), F32)],
        scratch_shapes=[
            pltpu.VMEM((POOL_HIST + ts, pw), F32),
            pltpu.VMEM((CONV_HIST + ts, cw), F32),
            pltpu.VMEM((ts, pw), BF16),
            pltpu.VMEM((ts, cw), F32),
        ],
        compiler_params=pltpu.CompilerParams(
            dimension_semantics=("arbitrary", "arbitrary"), vmem_limit_bytes=VMEM_LIMIT_MIXER),
        name="mixer",
    )(x, w_in, pool_w, pool_scale, conv_dw, cln_g, cln_b, conv_w_out, w_out, ln1_g, ln1_b,
      w_router_t)


ROUTE_BASE = 0
ROUTE_NCH = 1
ROUTE_CHUNK = 2
ROUTE_START = 3


def _route_kernel(lt_ref, bias_ref, w_ref, lpos_ref, bpos_ref, tab_ref, cnt_ref, carry):
    n_exp, tt = lt_ref.shape
    neg = -jnp.inf

    @pl.when(pl.program_id(0) == 0)
    def _():
        carry[...] = jnp.zeros_like(carry)

    scores = jax.nn.sigmoid(lt_ref[...])
    sel = scores + bias_ref[...]
    sel3 = sel.reshape(N_GROUPS, EXPERTS_PER_GROUP, tt)
    io3 = lax.broadcasted_iota(I32, sel3.shape, 1)
    m1 = jnp.max(sel3, axis=1, keepdims=True)
    i1 = jnp.min(jnp.where(sel3 == m1, io3, EXPERTS_PER_GROUP), axis=1, keepdims=True)
    m2 = jnp.max(jnp.where(io3 == i1, neg, sel3), axis=1, keepdims=True)
    gscore = m1 + m2
    iog = lax.broadcasted_iota(I32, gscore.shape, 0)
    gsel = jnp.zeros(gscore.shape, F32)
    for _ in range(TOPK_GROUPS):
        m = jnp.max(gscore, axis=0, keepdims=True)
        gi = jnp.min(jnp.where(gscore == m, iog, N_GROUPS), axis=0, keepdims=True)
        hit = iog == gi
        gsel = jnp.where(hit, 1.0, gsel)
        gscore = jnp.where(hit, neg, gscore)
    val = jnp.where(gsel > 0.5, sel3, neg).reshape(n_exp, tt)
    ioe = lax.broadcasted_iota(I32, (n_exp, tt), 0)
    member = jnp.zeros((n_exp, tt), F32)
    idxs, ws = [], []
    for _ in range(TOP_K):
        m = jnp.max(val, axis=0, keepdims=True)
        ei = jnp.min(jnp.where(val == m, ioe, n_exp), axis=0, keepdims=True)
        hit = ioe == ei
        idxs.append(ei)
        ws.append(jnp.sum(jnp.where(hit, scores, 0.0), axis=0, keepdims=True))
        member = jnp.where(hit, 1.0, member)
        val = jnp.where(hit, neg, val)
    w = jnp.concatenate(ws, axis=0)
    w_ref[...] = w / jnp.sum(w, axis=0, keepdims=True) * ROUTED_SCALE

    mb = member.astype(BF16)
    before = (lax.broadcasted_iota(I32, (tt, tt), 0) < lax.broadcasted_iota(I32, (tt, tt), 1))
    rank_in_tile = jnp.dot(mb, before.astype(BF16), preferred_element_type=F32)
    c_col = jnp.sum(member, axis=1, keepdims=True)
    c_row = lax.dot_general(jnp.ones((SUBLANES, tt), BF16), mb, (((1,), (1,)), ((), ())),
                            preferred_element_type=F32)

    def even(c):
        return c + (c - 2.0 * jnp.floor(c * 0.5))

    def chunks(c):
        return jnp.maximum(jnp.floor((c + (RUN_ROWS - 1)) * (1.0 / RUN_ROWS)), 1.0)

    ee0 = lax.broadcasted_iota(I32, (n_exp, n_exp), 0)
    ee1 = lax.broadcasted_iota(I32, (n_exp, n_exp), 1)
    lower = (ee1 < ee0).astype(BF16)
    upper = (ee0 < ee1).astype(BF16)

    def prefix_col(v):
        return jnp.dot(lower, jnp.broadcast_to(v, (n_exp, LANES)).astype(BF16),
                       preferred_element_type=F32)[:, 0:1]

    def prefix_row(v):
        return jnp.dot(v.astype(BF16), upper, preferred_element_type=F32)

    chunk_col = prefix_col(chunks(c_col))
    start_col = prefix_col(even(c_col))
    bfull = chunk_col * RUN_ROWS + rank_in_tile
    lfull = start_col + rank_in_tile
    bpos_ref[...] = jnp.concatenate(
        [jnp.sum(jnp.where(ioe == ei, bfull, 0.0), axis=0, keepdims=True) for ei in idxs],
        axis=0).astype(I32)
    lpos_ref[...] = jnp.concatenate(
        [jnp.sum(jnp.where(ioe == ei, lfull, 0.0), axis=0, keepdims=True) for ei in idxs],
        axis=0).astype(I32)
    nch_row = chunks(c_row)
    tab_ref[0] = jnp.concatenate(
        [carry[0:1], nch_row[0:1], prefix_row(nch_row)[0:1], prefix_row(even(c_row))[0:1],
         jnp.zeros((SUBLANES - 4, n_exp), F32)], axis=0)
    carry[...] = carry[...] + even(c_row)
    cnt_ref[...] = carry[...]


def _route_call(logits_t, bias_col):
    n_exp, t = logits_t.shape
    tt = TILE
    return pl.pallas_call(
        _route_kernel,
        grid=(t // tt,),
        in_specs=[pl.BlockSpec((n_exp, tt), lambda i: (0, i)),
                  pl.BlockSpec((n_exp, 1), lambda i: (0, 0))],
        out_specs=[pl.BlockSpec((TOP_K, tt), lambda i: (0, i)),
                   pl.BlockSpec((TOP_K, tt), lambda i: (0, i)),
                   pl.BlockSpec((TOP_K, tt), lambda i: (0, i)),
                   pl.BlockSpec((1, SUBLANES, n_exp), lambda i: (i, 0, 0)),
                   pl.BlockSpec((SUBLANES, n_exp), lambda i: (0, 0))],
        out_shape=[jax.ShapeDtypeStruct((TOP_K, t), F32),
                   jax.ShapeDtypeStruct((TOP_K, t), I32),
                   jax.ShapeDtypeStruct((TOP_K, t), I32),
                   jax.ShapeDtypeStruct((t // tt, SUBLANES, n_exp), F32),
                   jax.ShapeDtypeStruct((SUBLANES, n_exp), F32)],
        scratch_shapes=[pltpu.VMEM((SUBLANES, n_exp), F32)],
        compiler_params=pltpu.CompilerParams(
            dimension_semantics=("arbitrary",), vmem_limit_bytes=VMEM_LIMIT_OTHER),
        name="route",
    )(logits_t, bias_col)


TAB_SLOT = 0
TAB_NCH = 1
TAB_SRC = 2
TAB_META = 3


def _start_chunks(tab_ref, n_exp, src_step, make_copy):
    def first(e, c):
        make_copy(tab_ref[0, TAB_SRC, e], tab_ref[0, TAB_SLOT, e]).start()
        return c

    lax.fori_loop(0, n_exp, first, 0, unroll=8)

    @pl.when(tab_ref[0, TAB_META, 1] > 1)
    def _():
        def extra(e, c):
            def one(i, c2):
                make_copy(tab_ref[0, TAB_SRC, e] + i * src_step,
                          tab_ref[0, TAB_SLOT, e] + i * RUN_ROWS).start()
                return c2
            lax.fori_loop(1, tab_ref[0, TAB_NCH, e], one, 0)
            return c
        lax.fori_loop(0, n_exp, extra, 0)


def _wait_chunks(tab_ref, make_copy):
    def wait(i, c):
        make_copy(0, 0).wait()
        return c
    lax.fori_loop(0, tab_ref[0, TAB_META, 0], wait, 0)


def _dispatch_kernel(zflag_ref, tab_ref, prev_tab_ref, lpos_ref, x1_ref, xs_hbm,
                     sbuf, zbuf, zsem, sem):
    tt = x1_ref.shape[0]
    n_blocks = zflag_ref.shape[0]
    n_exp = tab_ref.shape[2]
    sub = zbuf.shape[0] // EXPERT_BLOCK
    chunk_rows = RUN_ROWS * sub
    i = pl.program_id(0)

    def zero_copy(blk):
        return pltpu.make_async_copy(
            zbuf, xs_hbm.at[pl.ds(pl.multiple_of(blk * zbuf.shape[0], zbuf.shape[0]),
                                  zbuf.shape[0])], zsem)

    @pl.when(i == 0)
    def _():
        zbuf[...] = jnp.zeros_like(zbuf)

        def start(blk, c):
            @pl.when(zflag_ref[blk] == 1)
            def _():
                zero_copy(blk).start()
            return c

        def wait(blk, c):
            @pl.when(zflag_ref[blk] == 1)
            def _():
                zero_copy(blk).wait()
            return c

        lax.fori_loop(0, n_blocks, start, 0)
        lax.fori_loop(0, n_blocks, wait, 0)

    def make_copy_from(buf):
        def make_copy(row, slot):
            return pltpu.make_async_copy(
                buf.at[pl.ds(pl.multiple_of(row * sub, 2 * sub), chunk_rows)],
                xs_hbm.at[pl.ds(pl.multiple_of(slot * sub, 2 * sub), chunk_rows)], sem)
        return make_copy

    xb = x1_ref[...].astype(BF16)

    def sort_into(buf):
        for rb in range(SORT_ROWS // SORT_BLOCK):
            ior = lax.broadcasted_iota(I32, (SORT_BLOCK, tt), 0) + rb * SORT_BLOCK
            p = jnp.zeros((SORT_BLOCK, tt), F32)
            for k in range(TOP_K):
                p = jnp.where(ior == lpos_ref[k:k + 1, :], 1.0, p)
            _store_packed_rows(buf, rb * SORT_BLOCK,
                               jnp.dot(p.astype(BF16), xb, preferred_element_type=F32))

    for s in range(2):
        @pl.when(i % 2 == s)
        def _():
            sort_into(sbuf.at[s])

    @pl.when(i > 0)
    def _():
        _wait_chunks(prev_tab_ref, make_copy_from(sbuf.at[0]))

    for s in range(2):
        @pl.when(i % 2 == s)
        def _():
            _start_chunks(tab_ref, n_exp, RUN_ROWS, make_copy_from(sbuf.at[s]))

    @pl.when(i == pl.num_programs(0) - 1)
    def _():
        _wait_chunks(tab_ref, make_copy_from(sbuf.at[0]))


def _dispatch_call(zflag, tab, lpos, x1, n_slots):
    t, d = x1.shape
    tt = TILE
    n_exp = tab.shape[2]
    sub = d // 2 // LANES
    tab_spec = lambda f: pl.BlockSpec((1, 4, n_exp), f, memory_space=pltpu.SMEM)
    grid_spec = pltpu.PrefetchScalarGridSpec(
        num_scalar_prefetch=1,
        grid=(t // tt,),
        in_specs=[tab_spec(lambda i, zf: (i, 0, 0)),
                  tab_spec(lambda i, zf: (jnp.maximum(i - 1, 0), 0, 0)),
                  pl.BlockSpec((TOP_K, tt), lambda i, zf: (0, i)),
                  pl.BlockSpec((tt, d), lambda i, zf: (i, 0))],
        out_specs=pl.BlockSpec(memory_space=pl.ANY),
        scratch_shapes=[pltpu.VMEM((2, SORT_ROWS * sub, LANES), U32),
                        pltpu.VMEM((EXPERT_BLOCK * sub, LANES), U32),
                        pltpu.SemaphoreType.DMA(()),
                        pltpu.SemaphoreType.DMA(())],
    )
    return pl.pallas_call(
        _dispatch_kernel,
        grid_spec=grid_spec,
        out_shape=jax.ShapeDtypeStruct((n_slots * sub, LANES), U32),
        compiler_params=pltpu.CompilerParams(
            dimension_semantics=("arbitrary",), vmem_limit_bytes=VMEM_LIMIT_OTHER),
        name="dispatch",
    )(zflag, tab, tab, lpos, x1)


def _expert_kernel(first_ref, nblk_ref, tail_ref, xs_hbm, wg_ref, wu_ref, wd_ref, ys_hbm,
                   wgu_s, wd_s, xbuf, ybuf, isem, osem):
    e = pl.program_id(0)
    last = pl.num_programs(0) - 1
    hid = wg_ref.shape[2]
    blk_rows = xbuf.shape[1]
    sub = blk_rows // EXPERT_BLOCK
    n = nblk_ref[e]
    first = first_ref[e]

    def rows_of(blk):
        return pl.ds(pl.multiple_of(blk * blk_rows, blk_rows), blk_rows)

    def in_copy(blk, s):
        return pltpu.make_async_copy(xs_hbm.at[rows_of(blk)], xbuf.at[s], isem.at[s])

    def out_copy(blk, s):
        return pltpu.make_async_copy(ybuf.at[s], ys_hbm.at[rows_of(blk)], osem.at[s])

    @pl.when(e == 0)
    def _():
        in_copy(first, 0).start()

    wgu_s[:, 0:hid] = wg_ref[0].astype(BF16)
    wgu_s[:, hid:2 * hid] = wu_ref[0].astype(BF16)
    wd_s[...] = wd_ref[0].astype(BF16)

    def compute(s):
        xb = _load_packed_rows(xbuf.at[s], 0, EXPERT_BLOCK, sub)
        gu = jnp.dot(xb, wgu_s[...], preferred_element_type=F32)
        h = _silu(gu[:, :hid]) * gu[:, hid:]
        _store_packed_rows(ybuf.at[s], 0,
                           jnp.dot(h.astype(BF16), wd_s[...], preferred_element_type=F32))

    def pair(p, c):
        b0 = 2 * p
        in_copy(first + b0, 0).wait()

        @pl.when(b0 + 1 < n)
        def _():
            in_copy(first + b0 + 1, 1).start()

        @pl.when(p >= 1)
        def _():
            out_copy(first, 0).wait()

        compute(0)
        out_copy(first + b0, 0).start()

        @pl.when(b0 + 1 < n)
        def _():
            in_copy(first + b0 + 1, 1).wait()

            @pl.when(b0 + 2 < n)
            def _():
                in_copy(first + b0 + 2, 0).start()

            @pl.when(p >= 1)
            def _():
                out_copy(first, 1).wait()

            compute(1)
            out_copy(first + b0 + 1, 1).start()
        return c

    lax.fori_loop(0, (n + 1) // 2, pair, 0)
    out_copy(first, 0).wait()

    @pl.when(n >= 2)
    def _():
        out_copy(first, 1).wait()

    @pl.when(e < last)
    def _():
        in_copy(first_ref[e + 1], 0).start()

    @pl.when(e == last)
    def _():
        ybuf[0] = jnp.zeros(ybuf.shape[1:], ybuf.dtype)

        def start(b, c):
            out_copy(b, 0).start()
            return c

        def wait(b, c):
            out_copy(b, 0).wait()
            return c

        lax.fori_loop(tail_ref[0], tail_ref[1], start, 0)
        lax.fori_loop(tail_ref[0], tail_ref[1], wait, 0)


def _expert_call(first_blk, n_blk, tail, xs, w_gate_e, w_up_e, w_down_e):
    n_exp, d, hid = w_gate_e.shape
    blk_rows = EXPERT_BLOCK * (d // 2 // LANES)
    grid_spec = pltpu.PrefetchScalarGridSpec(
        num_scalar_prefetch=3,
        grid=(n_exp,),
        in_specs=[pl.BlockSpec(memory_space=pl.ANY),
                  pl.BlockSpec((1, d, hid), lambda e, *_: (e, 0, 0)),
                  pl.BlockSpec((1, d, hid), lambda e, *_: (e, 0, 0)),
                  pl.BlockSpec((1, hid, d), lambda e, *_: (e, 0, 0))],
        out_specs=pl.BlockSpec(memory_space=pl.ANY),
        scratch_shapes=[pltpu.VMEM((d, 2 * hid), BF16), pltpu.VMEM((hid, d), BF16),
                        pltpu.VMEM((2, blk_rows, LANES), U32),
                        pltpu.VMEM((2, blk_rows, LANES), U32),
                        pltpu.SemaphoreType.DMA((2,)), pltpu.SemaphoreType.DMA((2,))],
    )
    return pl.pallas_call(
        _expert_kernel,
        grid_spec=grid_spec,
        out_shape=jax.ShapeDtypeStruct(xs.shape, U32),
        compiler_params=pltpu.CompilerParams(
            dimension_semantics=("arbitrary",), vmem_limit_bytes=VMEM_LIMIT_OTHER),
        name="experts",
    )(first_blk, n_blk, tail, xs, w_gate_e, w_up_e, w_down_e)


def _combine_kernel(tab_ref, next_tab_ref, x1_ref, bpos_ref, w_ref, ys_hbm, wsgu_ref, wsd_ref,
                    g2_ref, b2_ref, out_ref, buf, acc, sem, *, alpha):
    tt = x1_ref.shape[0]
    hid = wsd_ref.shape[0]
    sub = buf.shape[1] // BUF_ROWS
    chunk_rows = RUN_ROWS * sub
    n_exp = tab_ref.shape[2]
    i = pl.program_id(0)

    def make_copy_into(s):
        def make_copy(chunk, slot):
            return pltpu.make_async_copy(
                ys_hbm.at[pl.ds(pl.multiple_of(slot * sub, 2 * sub), chunk_rows)],
                buf.at[s, pl.ds(pl.multiple_of(chunk * chunk_rows, chunk_rows), chunk_rows)],
                sem.at[s])
        return make_copy

    @pl.when(i == 0)
    def _():
        buf[...] = jnp.zeros_like(buf)
        _start_chunks(tab_ref, n_exp, 1, make_copy_into(0))

    for s in range(2):
        @pl.when((i % 2 == s) & (i + 1 < pl.num_programs(0)))
        def _():
            _start_chunks(next_tab_ref, n_exp, 1, make_copy_into(1 - s))

    x1 = x1_ref[...]
    gu = jnp.dot(x1.astype(BF16), wsgu_ref[...], preferred_element_type=F32)
    hs = _silu(gu[:, :hid]) * gu[:, hid:]
    acc[...] = jnp.dot(hs.astype(BF16), wsd_ref[...], preferred_element_type=F32)

    n_rows = tab_ref[0, TAB_META, 0] * RUN_ROWS
    bpos = bpos_ref[...]
    w = w_ref[...]
    for s in range(2):
        @pl.when(i % 2 == s)
        def _():
            _wait_chunks(tab_ref, make_copy_into(s))
            for kb in range(BUF_ROWS // SORT_BLOCK):
                @pl.when(kb * SORT_BLOCK < n_rows)
                def _():
                    iol = lax.broadcasted_iota(I32, (tt, SORT_BLOCK), 1) + kb * SORT_BLOCK
                    q = jnp.zeros((tt, SORT_BLOCK), F32)
                    for k in range(TOP_K):
                        q = jnp.where(iol == bpos[:, k:k + 1], w[:, k:k + 1], q)
                    rows = _load_packed_rows(buf.at[s], kb * SORT_BLOCK, SORT_BLOCK, sub)
                    acc[...] += jnp.dot(q.astype(BF16), rows, preferred_element_type=F32)

    out_ref[...] = _layernorm(alpha * x1 + acc[...], g2_ref[...], b2_ref[...])


def _combine_call(tab, x1, bpos_tok, w_tok, ys, ws_gu, ws_d, ln2_g, ln2_b, *, alpha):
    t, d = x1.shape
    tt = TILE
    n_tiles = t // tt
    n_exp = tab.shape[2]

    def const(shape):
        return pl.BlockSpec(shape, lambda i: (0,) * len(shape))

    tab_spec = lambda f: pl.BlockSpec((1, 4, n_exp), f, memory_space=pltpu.SMEM)
    return pl.pallas_call(
        functools.partial(_combine_kernel, alpha=alpha),
        grid=(n_tiles,),
        in_specs=[tab_spec(lambda i: (i, 0, 0)),
                  tab_spec(lambda i: (jnp.minimum(i + 1, n_tiles - 1), 0, 0)),
                  pl.BlockSpec((tt, d), lambda i: (i, 0)),
                  pl.BlockSpec((tt, TOP_K), lambda i: (i, 0)),
                  pl.BlockSpec((tt, TOP_K), lambda i: (i, 0)),
                  pl.BlockSpec(memory_space=pl.ANY),
                  const(ws_gu.shape), const(ws_d.shape), const(ln2_g.shape), const(ln2_b.shape)],
        out_specs=pl.BlockSpec((tt, d), lambda i: (i, 0)),
        out_shape=jax.ShapeDtypeStruct((t, d), F32),
        scratch_shapes=[pltpu.VMEM((2, BUF_ROWS * (d // 2 // LANES), LANES), U32),
                        pltpu.VMEM((tt, d), F32),
                        pltpu.SemaphoreType.DMA((2,))],
        compiler_params=pltpu.CompilerParams(
            dimension_semantics=("arbitrary",), vmem_limit_bytes=VMEM_LIMIT_OTHER),
        name="combine",
    )(tab, tab, x1, bpos_tok, w_tok, ys, ws_gu, ws_d, ln2_g, ln2_b)


def _layer(x, w_in, pool_w, pool_scale, conv_dw, conv_ln_g, conv_ln_b, conv_w_out, w_out,
           ln1_g, ln1_b, w_router, router_bias, w_gate_e, w_up_e, w_down_e,
           ws_gate, ws_up, ws_down, ln2_g, ln2_b, *, alpha):
    b, s, d = x.shape
    t = b * s
    row = lambda v: v.reshape(1, -1)
    x1, logits_t = _mixer_call(
        x, w_in.astype(BF16), pool_w.astype(BF16), row(pool_scale), conv_dw, row(conv_ln_g),
        row(conv_ln_b), conv_w_out.astype(BF16), w_out.astype(BF16), row(ln1_g), row(ln1_b),
        w_router.T.astype(BF16), alpha=alpha)
    w_t, lpos_t, bpos_t, rtab, counts = _route_call(logits_t, router_bias.reshape(-1, 1))

    n = t * TOP_K
    n_blocks = -(-(n + (t // TILE) * N_EXPERTS + N_EXPERTS * (EXPERT_BLOCK - 1 + RUN_ROWS))
                 // EXPERT_BLOCK)
    n_slots = n_blocks * EXPERT_BLOCK
    counts = counts[0].astype(I32)
    n_blk = (counts + RUN_ROWS + EXPERT_BLOCK - 1) // EXPERT_BLOCK
    end_blk = jnp.cumsum(n_blk)
    first_blk = end_blk - n_blk
    pad_start = first_blk * EXPERT_BLOCK
    n_used = end_blk[-1]
    blk = jnp.arange(n_blocks, dtype=I32)
    block_e = jnp.minimum(jnp.sum((end_blk[None, :] <= blk[:, None]).astype(I32), axis=1),
                          N_EXPERTS - 1)
    real_end = (pad_start + counts)[block_e]
    zflag = ((blk >= n_used) | ((blk + 1) * EXPERT_BLOCK > real_end)).astype(I32)
    tail = jnp.stack([n_used, jnp.asarray(n_blocks, I32)])

    rtab = rtab.astype(I32)
    nch = rtab[:, ROUTE_NCH, :]
    meta = jnp.zeros_like(nch).at[:, 0].set(jnp.sum(nch, axis=1)).at[:, 1].set(jnp.max(nch, axis=1))
    slot = rtab[:, ROUTE_BASE, :] + pad_start[None, :]
    tab_d = jnp.stack([slot, nch, rtab[:, ROUTE_START, :], meta], axis=1)
    tab_c = jnp.stack([slot, nch, rtab[:, ROUTE_CHUNK, :], meta], axis=1)

    xs = _dispatch_call(zflag, tab_d, lpos_t, x1, n_slots)
    ys = _expert_call(first_blk, n_blk, tail, xs, w_gate_e, w_up_e, w_down_e)
    ws_gu = jnp.concatenate([ws_gate, ws_up], axis=1).astype(BF16)
    out = _combine_call(tab_c, x1, bpos_t.T, w_t.T, ys, ws_gu, ws_down.astype(BF16), row(ln2_g),
                        row(ln2_b), alpha=alpha)
    return out.reshape(b, s, d)


def kernel(x, w_in, pool_w, pool_scale, conv_dw, conv_ln_g, conv_ln_b, conv_w_out, w_out, ln1_g, ln1_b, w_router, router_bias, w_gate_e, w_up_e, w_down_e, ws_gate, ws_up, ws_down, ln2_g, ln2_b):
    depth = w_in.shape[0]
    alpha = (2.0 * depth) ** 0.25
    for l in range(depth):
        x = _layer(x, w_in[l], pool_w[l], pool_scale[l], conv_dw[l], conv_ln_g[l], conv_ln_b[l],
                   conv_w_out[l], w_out[l], ln1_g[l], ln1_b[l], w_router[l], router_bias[l],
                   w_gate_e[l], w_up_e[l], w_down_e[l], ws_gate[l], ws_up[l], ws_down[l],
                   ln2_g[l], ln2_b[l], alpha=alpha)
    return x
```

```python
import functools

import jax
import jax.numpy as jnp
from jax import lax
from jax.experimental import pallas as pl
from jax.experimental.pallas import tpu as pltpu

F32 = jnp.float32
BF16 = jnp.bfloat16
I32 = jnp.int32
U32 = jnp.uint32

POOL_GROUPS = 4
POOL_WINDOWS = (2, 4, 8, 16)
CONV_KERNEL = 31
N_EXPERTS = 256
TOP_K = 8
N_GROUPS = 8
TOPK_GROUPS = 4
EXPERTS_PER_GROUP = N_EXPERTS // N_GROUPS
ROUTED_SCALE = 2.5
LN_EPS = 1e-5

SUBLANES = 8
LANES = 128
MIX_ROWS = 512
POOL_CHUNK = 32
CONV_CHUNK = 64
CONV_COLS = 256
POOL_HIST = 16
CONV_HIST = 32
TILE = 256
RUN_ROWS = 16
SORT_BLOCK = 512
MAX_CHUNKS = N_EXPERTS + TILE * TOP_K // RUN_ROWS
BUF_ROWS = MAX_CHUNKS * RUN_ROWS
SORT_ROWS = -(-(TILE * TOP_K + N_EXPERTS + RUN_ROWS) // SORT_BLOCK) * SORT_BLOCK
EXPERT_BLOCK = 256
EXPERT_RING = 4
VMEM_LIMIT_MIXER = 56 * 1024 * 1024
VMEM_LIMIT_OTHER = 48 * 1024 * 1024


def _layernorm(z, g, b):
    mu = jnp.mean(z, axis=-1, keepdims=True)
    d = z - mu
    var = jnp.mean(d * d, axis=-1, keepdims=True)
    return d * lax.rsqrt(var + LN_EPS) * g + b


def _silu(v):
    return v * jax.nn.sigmoid(v)


def _store_packed_rows(ref, row0, v):
    n, d = v.shape
    half = d // 2
    sub = half // LANES
    for j in range(sub):
        words = pltpu.pack_elementwise(
            [v[:, j * LANES:(j + 1) * LANES], v[:, half + j * LANES:half + (j + 1) * LANES]],
            packed_dtype=BF16)
        ref[pl.ds(row0 * sub + j, n, stride=sub), :] = words


def _load_packed_rows(ref, row0, n, sub):
    los, his = [], []
    for j in range(sub):
        words = ref[pl.ds(row0 * sub + j, n, stride=sub), :]
        los.append(pltpu.unpack_elementwise(words, index=0, packed_dtype=BF16,
                                            unpacked_dtype=F32).astype(BF16))
        his.append(pltpu.unpack_elementwise(words, index=1, packed_dtype=BF16,
                                            unpacked_dtype=F32).astype(BF16))
    return jnp.concatenate(los + his, axis=1)


def _mixer_kernel(x_ref, w_in_ref, pool_w_ref, pool_scale_ref, dw_ref, cg_ref, cb_ref, cwo_ref,
                  w_out_ref, g1_ref, b1_ref, wr_ref, x1_ref, lt_ref,
                  ubuf, vbuf, rbuf, cvbuf, *, alpha):
    ts = x_ref.shape[1]
    d_model = x_ref.shape[2]
    pw = ubuf.shape[1]
    cw = vbuf.shape[1]
    gi = pw // POOL_GROUPS
    si = pl.program_id(1)

    @pl.when(si == 0)
    def _():
        ubuf[0:POOL_HIST, :] = jnp.zeros((POOL_HIST, pw), F32)
        vbuf[0:CONV_HIST, :] = jnp.zeros((CONV_HIST, cw), F32)

    x = x_ref[0]
    xb = x.astype(BF16)
    ubuf[POOL_HIST:POOL_HIST + ts, :] = jnp.dot(xb, w_in_ref[:, 0:pw], preferred_element_type=F32)
    a = jnp.dot(xb, w_in_ref[:, pw:pw + 2 * cw], preferred_element_type=F32)
    vbuf[CONV_HIST:CONV_HIST + ts, :] = a[:, :cw] * jax.nn.sigmoid(a[:, cw:])
    pos0 = si * ts

    for c in range(ts // POOL_CHUNK):
        r0 = c * POOL_CHUNK
        pos = pos0 + r0 + lax.broadcasted_iota(I32, (POOL_CHUNK, 1), 0)
        for g, w in enumerate(POOL_WINDOWS):
            cols = slice(g * gi, (g + 1) * gi)
            cur = ubuf[pl.ds(POOL_HIST + r0, POOL_CHUNK), cols]
            s = cur
            for j in range(1, w):
                s = s + ubuf[pl.ds(POOL_HIST + r0 - j, POOL_CHUNK), cols]
            cnt = jnp.minimum(pos + 1, w).astype(F32)
            rbuf[pl.ds(r0, POOL_CHUNK), cols] = (s / cnt - cur).astype(BF16)

    for c in range(ts // CONV_CHUNK):
        r0 = c * CONV_CHUNK
        base = CONV_HIST + r0 - (CONV_KERNEL - 1)
        for c0 in range(0, cw, CONV_COLS):
            cols = slice(c0, c0 + CONV_COLS)
            acc = None
            for rho in range(SUBLANES):
                taps = range(rho, CONV_KERNEL, SUBLANES)
                win = vbuf[pl.ds(base + rho, CONV_CHUNK + taps[-1] - rho), cols]
                for k in taps:
                    term = dw_ref[k:k + 1, cols] * win[k - rho:k - rho + CONV_CHUNK, :]
                    acc = term if acc is None else acc + term
            cvbuf[pl.ds(r0, CONV_CHUNK), cols] = acc

    ubuf[0:POOL_HIST, :] = ubuf[ts:ts + POOL_HIST, :]
    vbuf[0:CONV_HIST, :] = vbuf[ts:ts + CONV_HIST, :]

    y_pool = jnp.concatenate(
        [jnp.dot(rbuf[:, g * gi:(g + 1) * gi], pool_w_ref[g], preferred_element_type=F32)
         for g in range(POOL_GROUPS)], axis=1) * pool_scale_ref[...]
    conv = _silu(_layernorm(cvbuf[...], cg_ref[...], cb_ref[...]))
    y_conv = jnp.dot(conv.astype(BF16), cwo_ref[...], preferred_element_type=F32)
    gates = jnp.dot(xb, w_in_ref[:, pw + 2 * cw:], preferred_element_type=F32)
    merged = (jax.nn.sigmoid(gates[:, :d_model]) * y_pool
              + jax.nn.sigmoid(gates[:, d_model:]) * y_conv)
    m = jnp.dot(merged.astype(BF16), w_out_ref[...], preferred_element_type=F32)
    x1 = _layernorm(alpha * x + m, g1_ref[...], b1_ref[...])
    x1_ref[...] = x1
    lt_ref[...] = lax.dot_general(wr_ref[...], x1.astype(BF16), (((1,), (1,)), ((), ())),
                                  preferred_element_type=F32)


def _mixer_call(x, w_in, pool_w, pool_scale, conv_dw, cln_g, cln_b, conv_w_out, w_out,
                ln1_g, ln1_b, w_router_t, *, alpha):
    b, s, d = x.shape
    ts = MIX_ROWS
    ns = s // ts
    pw = pool_w.shape[0] * pool_w.shape[1]
    cw = conv_dw.shape[1]
    n_exp = w_router_t.shape[0]

    def const(shape):
        return pl.BlockSpec(shape, lambda bi, si: (0,) * len(shape))

    return pl.pallas_call(
        functools.partial(_mixer_kernel, alpha=alpha),
        grid=(b, ns),
        in_specs=[
            pl.BlockSpec((1, ts, d), lambda bi, si: (bi, si, 0)),
            const(w_in.shape), const(pool_w.shape), const(pool_scale.shape), const(conv_dw.shape),
            const(cln_g.shape), const(cln_b.shape), const(conv_w_out.shape), const(w_out.shape),
            const(ln1_g.shape), const(ln1_b.shape), const(w_router_t.shape),
        ],
        out_specs=[
            pl.BlockSpec((ts, d), lambda bi, si: (bi * ns + si, 0)),
            pl.BlockSpec((n_exp, ts), lambda bi, si: (0, bi * ns + si)),
        ],
        out_shape=[jax.ShapeDtypeStruct((b * s, d), F32),
                   jax.ShapeDtypeStruct((n_exp, b * s), F32)],
        scratch_shapes=[
            pltpu.VMEM((POOL_HIST + ts, pw), F32),
            pltpu.VMEM((CONV_HIST + ts, cw), F32),
            pltpu.VMEM((ts, pw), BF16),
            pltpu.VMEM((ts, cw), F32),
        ],
        compiler_params=pltpu.CompilerParams(
            dimension_semantics=("arbitrary", "arbitrary"), vmem_limit_bytes=VMEM_LIMIT_MIXER),
        name="mixer",
    )(x, w_in, pool_w, pool_scale, conv_dw, cln_g, cln_b, conv_w_out, w_out, ln1_g, ln1_b,
      w_router_t)


ROUTE_BASE = 0
ROUTE_NCH = 1
ROUTE_CHUNK = 2
ROUTE_START = 3


def _route_kernel(lt_ref, bias_ref, w_ref, lpos_ref, bpos_ref, tab_ref, cnt_ref, carry):
    n_exp, tt = lt_ref.shape
    neg = -jnp.inf

    @pl.when(pl.program_id(0) == 0)
    def _():
        carry[...] = jnp.zeros_like(carry)

    scores = jax.nn.sigmoid(lt_ref[...])
    sel = scores + bias_ref[...]
    sel3 = sel.reshape(N_GROUPS, EXPERTS_PER_GROUP, tt)
    io3 = lax.broadcasted_iota(I32, sel3.shape, 1)
    m1 = jnp.max(sel3, axis=1, keepdims=True)
    i1 = jnp.min(jnp.where(sel3 == m1, io3, EXPERTS_PER_GROUP), axis=1, keepdims=True)
    m2 = jnp.max(jnp.where(io3 == i1, neg, sel3), axis=1, keepdims=True)
    gscore = m1 + m2
    iog = lax.broadcasted_iota(I32, gscore.shape, 0)
    gsel = jnp.zeros(gscore.shape, F32)
    for _ in range(TOPK_GROUPS):
        m = jnp.max(gscore, axis=0, keepdims=True)
        gi = jnp.min(jnp.where(gscore == m, iog, N_GROUPS), axis=0, keepdims=True)
        hit = iog == gi
        gsel = jnp.where(hit, 1.0, gsel)
        gscore = jnp.where(hit, neg, gscore)
    val = jnp.where(gsel > 0.5, sel3, neg).reshape(n_exp, tt)
    ioe = lax.broadcasted_iota(I32, (n_exp, tt), 0)
    member = jnp.zeros((n_exp, tt), F32)
    idxs, ws = [], []
    for _ in range(TOP_K):
        m = jnp.max(val, axis=0, keepdims=True)
        ei = jnp.min(jnp.where(val == m, ioe, n_exp), axis=0, keepdims=True)
        hit = ioe == ei
        idxs.append(ei)
        ws.append(jnp.sum(jnp.where(hit, scores, 0.0), axis=0, keepdims=True))
        member = jnp.where(hit, 1.0, member)
        val = jnp.where(hit, neg, val)
    w = jnp.concatenate(ws, axis=0)
    w_ref[...] = w / jnp.sum(w, axis=0, keepdims=True) * ROUTED_SCALE

    mb = member.astype(BF16)
    before = (lax.broadcasted_iota(I32, (tt, tt), 0) < lax.broadcasted_iota(I32, (tt, tt), 1))
    rank_in_tile = jnp.dot(mb, before.astype(BF16), preferred_element_type=F32)
    c_col = jnp.sum(member, axis=1, keepdims=True)
    c_row = lax.dot_general(jnp.ones((SUBLANES, tt), BF16), mb, (((1,), (1,)), ((), ())),
                            preferred_element_type=F32)

    def even(c):
        return c + (c - 2.0 * jnp.floor(c * 0.5))

    def chunks(c):
        return jnp.maximum(jnp.floor((c + (RUN_ROWS - 1)) * (1.0 / RUN_ROWS)), 1.0)

    ee0 = lax.broadcasted_iota(I32, (n_exp, n_exp), 0)
    ee1 = lax.broadcasted_iota(I32, (n_exp, n_exp), 1)
    lower = (ee1 < ee0).astype(BF16)
    upper = (ee0 < ee1).astype(BF16)

    def prefix_col(v):
        return jnp.dot(lower, jnp.broadcast_to(v, (n_exp, LANES)).astype(BF16),
                       preferred_element_type=F32)[:, 0:1]

    def prefix_row(v):
        return jnp.dot(v.astype(BF16), upper, preferred_element_type=F32)

    chunk_col = prefix_col(chunks(c_col))
    start_col = prefix_col(even(c_col))
    bfull = chunk_col * RUN_ROWS + rank_in_tile
    lfull = start_col + rank_in_tile
    bpos_ref[...] = jnp.concatenate(
        [jnp.sum(jnp.where(ioe == ei, bfull, 0.0), axis=0, keepdims=True) for ei in idxs],
        axis=0).astype(I32)
    lpos_ref[...] = jnp.concatenate(
        [jnp.sum(jnp.where(ioe == ei, lfull, 0.0), axis=0, keepdims=True) for ei in idxs],
        axis=0).astype(I32)
    nch_row = chunks(c_row)
    tab_ref[0] = jnp.concatenate(
        [carry[0:1], nch_row[0:1], prefix_row(nch_row)[0:1], prefix_row(even(c_row))[0:1],
         jnp.zeros((SUBLANES - 4, n_exp), F32)], axis=0)
    carry[...] = carry[...] + even(c_row)
    cnt_ref[...] = carry[...]


def _route_call(logits_t, bias_col):
    n_exp, t = logits_t.shape
    tt = TILE
    return pl.pallas_call(
        _route_kernel,
        grid=(t // tt,),
        in_specs=[pl.BlockSpec((n_exp, tt), lambda i: (0, i)),
                  pl.BlockSpec((n_exp, 1), lambda i: (0, 0))],
        out_specs=[pl.BlockSpec((TOP_K, tt), lambda i: (0, i)),
                   pl.BlockSpec((TOP_K, tt), lambda i: (0, i)),
                   pl.BlockSpec((TOP_K, tt), lambda i: (0, i)),
                   pl.BlockSpec((1, SUBLANES, n_exp), lambda i: (i, 0, 0)),
                   pl.BlockSpec((SUBLANES, n_exp), lambda i: (0, 0))],
        out_shape=[jax.ShapeDtypeStruct((TOP_K, t), F32),
                   jax.ShapeDtypeStruct((TOP_K, t), I32),
                   jax.ShapeDtypeStruct((TOP_K, t), I32),
                   jax.ShapeDtypeStruct((t // tt, SUBLANES, n_exp), F32),
                   jax.ShapeDtypeStruct((SUBLANES, n_exp), F32)],
        scratch_shapes=[pltpu.VMEM((SUBLANES, n_exp), F32)],
        compiler_params=pltpu.CompilerParams(
            dimension_semantics=("arbitrary",), vmem_limit_bytes=VMEM_LIMIT_OTHER),
        name="route",
    )(logits_t, bias_col)


TAB_SLOT = 0
TAB_NCH = 1
TAB_SRC = 2
TAB_META = 3


def _start_chunks(tab_ref, n_exp, src_step, make_copy):
    def first(h, c):
        for u in range(2):
            e = 2 * h + u
            make_copy(tab_ref[0, TAB_SRC, e], tab_ref[0, TAB_SLOT, e]).start(priority=u)
        return c

    lax.fori_loop(0, n_exp // 2, first, 0, unroll=4)

    @pl.when(tab_ref[0, TAB_META, 1] > 1)
    def _():
        def extra(e, c):
            def one(i, c2):
                make_copy(tab_ref[0, TAB_SRC, e] + i * src_step,
                          tab_ref[0, TAB_SLOT, e] + i * RUN_ROWS).start()
                return c2
            lax.fori_loop(1, tab_ref[0, TAB_NCH, e], one, 0)
            return c
        lax.fori_loop(0, n_exp, extra, 0)


def _wait_chunks(tab_ref, make_copy):
    def wait(i, c):
        make_copy(0, 0).wait()
        return c
    lax.fori_loop(0, tab_ref[0, TAB_META, 0], wait, 0)


def _dispatch_kernel(zflag_ref, tab_ref, prev_tab_ref, lpos_ref, x1_ref, xs_hbm,
                     sbuf, zbuf, zsem, sem):
    tt = x1_ref.shape[0]
    n_blocks = zflag_ref.shape[0]
    n_exp = tab_ref.shape[2]
    sub = zbuf.shape[0] // EXPERT_BLOCK
    chunk_rows = RUN_ROWS * sub
    i = pl.program_id(0)

    def zero_copy(blk):
        return pltpu.make_async_copy(
            zbuf, xs_hbm.at[pl.ds(pl.multiple_of(blk * zbuf.shape[0], zbuf.shape[0]),
                                  zbuf.shape[0])], zsem)

    @pl.when(i == 0)
    def _():
        zbuf[...] = jnp.zeros_like(zbuf)

        def start(blk, c):
            @pl.when(zflag_ref[blk] == 1)
            def _():
                zero_copy(blk).start()
            return c

        def wait(blk, c):
            @pl.when(zflag_ref[blk] == 1)
            def _():
                zero_copy(blk).wait()
            return c

        lax.fori_loop(0, n_blocks, start, 0)
        lax.fori_loop(0, n_blocks, wait, 0)

    def make_copy_from(buf):
        def make_copy(row, slot):
            return pltpu.make_async_copy(
                buf.at[pl.ds(pl.multiple_of(row * sub, 2 * sub), chunk_rows)],
                xs_hbm.at[pl.ds(pl.multiple_of(slot * sub, 2 * sub), chunk_rows)], sem)
        return make_copy

    xb = x1_ref[...].astype(BF16)

    def sort_into(buf):
        for rb in range(SORT_ROWS // SORT_BLOCK):
            ior = lax.broadcasted_iota(I32, (SORT_BLOCK, tt), 0) + rb * SORT_BLOCK
            p = jnp.zeros((SORT_BLOCK, tt), F32)
            for k in range(TOP_K):
                p = jnp.where(ior == lpos_ref[k:k + 1, :], 1.0, p)
            _store_packed_rows(buf, rb * SORT_BLOCK,
                               jnp.dot(p.astype(BF16), xb, preferred_element_type=F32))

    for s in range(2):
        @pl.when(i % 2 == s)
        def _():
            sort_into(sbuf.at[s])

    @pl.when(i > 0)
    def _():
        _wait_chunks(prev_tab_ref, make_copy_from(sbuf.at[0]))

    for s in range(2):
        @pl.when(i % 2 == s)
        def _():
            _start_chunks(tab_ref, n_exp, RUN_ROWS, make_copy_from(sbuf.at[s]))

    @pl.when(i == pl.num_programs(0) - 1)
    def _():
        _wait_chunks(tab_ref, make_copy_from(sbuf.at[0]))


def _dispatch_call(zflag, tab, lpos, x1, n_slots):
    t, d = x1.shape
    tt = TILE
    n_exp = tab.shape[2]
    sub = d // 2 // LANES
    tab_spec = lambda f: pl.BlockSpec((1, 4, n_exp), f, memory_space=pltpu.SMEM)
    grid_spec = pltpu.PrefetchScalarGridSpec(
        num_scalar_prefetch=1,
        grid=(t // tt,),
        in_specs=[tab_spec(lambda i, zf: (i, 0, 0)),
                  tab_spec(lambda i, zf: (jnp.maximum(i - 1, 0), 0, 0)),
                  pl.BlockSpec((TOP_K, tt), lambda i, zf: (0, i)),
                  pl.BlockSpec((tt, d), lambda i, zf: (i, 0))],
        out_specs=pl.BlockSpec(memory_space=pl.ANY),
        scratch_shapes=[pltpu.VMEM((2, SORT_ROWS * sub, LANES), U32),
                        pltpu.VMEM((EXPERT_BLOCK * sub, LANES), U32),
                        pltpu.SemaphoreType.DMA(()),
                        pltpu.SemaphoreType.DMA(())],
    )
    return pl.pallas_call(
        _dispatch_kernel,
        grid_spec=grid_spec,
        out_shape=jax.ShapeDtypeStruct((n_slots * sub, LANES), U32),
        compiler_params=pltpu.CompilerParams(
            dimension_semantics=("arbitrary",), vmem_limit_bytes=VMEM_LIMIT_OTHER),
        name="dispatch",
    )(zflag, tab, tab, lpos, x1)


def _expert_kernel(first_ref, nblk_ref, tail_ref, xs_hbm, wg_ref, wu_ref, wd_ref, ys_hbm,
                   wgu_s, wd_s, xbuf, ybuf, isem, osem):
    e = pl.program_id(0)
    last = pl.num_programs(0) - 1
    hid = wg_ref.shape[2]
    blk_rows = xbuf.shape[0] // EXPERT_RING
    sub = blk_rows // EXPERT_BLOCK
    n_used = tail_ref[0]

    def rows_of(blk):
        return pl.ds(pl.multiple_of(blk * blk_rows, blk_rows), blk_rows)

    def in_copy(g):
        s = g % EXPERT_RING
        return pltpu.make_async_copy(xs_hbm.at[rows_of(g)], xbuf.at[rows_of(s)], isem.at[s])

    def out_copy(g):
        s = g % EXPERT_RING
        return pltpu.make_async_copy(ybuf.at[rows_of(s)], ys_hbm.at[rows_of(g)], osem.at[s])

    @pl.when(e == 0)
    def _():
        for g in range(EXPERT_RING - 1):
            in_copy(g).start()

    wgu_s[:, 0:hid] = wg_ref[0].astype(BF16)
    wgu_s[:, hid:2 * hid] = wu_ref[0].astype(BF16)
    wd_s[...] = wd_ref[0].astype(BF16)

    def block(b, c):
        g = first_ref[e] + b
        row0 = (g % EXPERT_RING) * EXPERT_BLOCK
        in_copy(g).wait()

        @pl.when(g + (EXPERT_RING - 1) < n_used)
        def _():
            in_copy(g + (EXPERT_RING - 1)).start()

        @pl.when(g >= EXPERT_RING)
        def _():
            out_copy(g).wait()

        xb = _load_packed_rows(xbuf, row0, EXPERT_BLOCK, sub)
        gu = jnp.dot(xb, wgu_s[...], preferred_element_type=F32)
        h = _silu(gu[:, :hid]) * gu[:, hid:]
        _store_packed_rows(ybuf, row0,
                           jnp.dot(h.astype(BF16), wd_s[...], preferred_element_type=F32))
        out_copy(g).start()
        return c

    lax.fori_loop(0, nblk_ref[e], block, 0)

    @pl.when(e == last)
    def _():
        for g in range(EXPERT_RING):
            out_copy(g).wait()
        ybuf[0:blk_rows, :] = jnp.zeros((blk_rows, ybuf.shape[1]), ybuf.dtype)

        def start(b, c):
            pltpu.make_async_copy(ybuf.at[rows_of(0)], ys_hbm.at[rows_of(b)], osem.at[0]).start()
            return c

        def wait(b, c):
            pltpu.make_async_copy(ybuf.at[rows_of(0)], ys_hbm.at[rows_of(b)], osem.at[0]).wait()
            return c

        lax.fori_loop(tail_ref[0], tail_ref[1], start, 0)
        lax.fori_loop(tail_ref[0], tail_ref[1], wait, 0)


def _expert_call(first_blk, n_blk, tail, xs, w_gate_e, w_up_e, w_down_e):
    n_exp, d, hid = w_gate_e.shape
    blk_rows = EXPERT_BLOCK * (d // 2 // LANES)
    grid_spec = pltpu.PrefetchScalarGridSpec(
        num_scalar_prefetch=3,
        grid=(n_exp,),
        in_specs=[pl.BlockSpec(memory_space=pl.ANY),
                  pl.BlockSpec((1, d, hid), lambda e, *_: (e, 0, 0)),
                  pl.BlockSpec((1, d, hid), lambda e, *_: (e, 0, 0)),
                  pl.BlockSpec((1, hid, d), lambda e, *_: (e, 0, 0))],
        out_specs=pl.BlockSpec(memory_space=pl.ANY),
        scratch_shapes=[pltpu.VMEM((d, 2 * hid), BF16), pltpu.VMEM((hid, d), BF16),
                        pltpu.VMEM((EXPERT_RING * blk_rows, LANES), U32),
                        pltpu.VMEM((EXPERT_RING * blk_rows, LANES), U32),
                        pltpu.SemaphoreType.DMA((EXPERT_RING,)),
                        pltpu.SemaphoreType.DMA((EXPERT_RING,))],
    )
    return pl.pallas_call(
        _expert_kernel,
        grid_spec=grid_spec,
        out_shape=jax.ShapeDtypeStruct(xs.shape, U32),
        compiler_params=pltpu.CompilerParams(
            dimension_semantics=("arbitrary",), vmem_limit_bytes=VMEM_LIMIT_OTHER),
        name="experts",
    )(first_blk, n_blk, tail, xs, w_gate_e, w_up_e, w_down_e)


def _combine_kernel(tab_ref, next_tab_ref, x1_ref, bpos_ref, w_ref, ys_hbm, wsgu_ref, wsd_ref,
                    g2_ref, b2_ref, out_ref, buf, acc, bp_s, wq_s, sem, *, alpha):
    tt = x1_ref.shape[0]
    hid = wsd_ref.shape[0]
    sub = buf.shape[1] // BUF_ROWS
    chunk_rows = RUN_ROWS * sub
    n_exp = tab_ref.shape[2]
    i = pl.program_id(0)

    def make_copy_into(s):
        def make_copy(chunk, slot):
            return pltpu.make_async_copy(
                ys_hbm.at[pl.ds(pl.multiple_of(slot * sub, 2 * sub), chunk_rows)],
                buf.at[s, pl.ds(pl.multiple_of(chunk * chunk_rows, chunk_rows), chunk_rows)],
                sem.at[s])
        return make_copy

    @pl.when(i == 0)
    def _():
        buf[...] = jnp.zeros_like(buf)
        _start_chunks(tab_ref, n_exp, 1, make_copy_into(0))

    for s in range(2):
        @pl.when((i % 2 == s) & (i + 1 < pl.num_programs(0)))
        def _():
            _start_chunks(next_tab_ref, n_exp, 1, make_copy_into(1 - s))

    x1 = x1_ref[...]
    gu = jnp.dot(x1.astype(BF16), wsgu_ref[...], preferred_element_type=F32)
    hs = _silu(gu[:, :hid]) * gu[:, hid:]
    acc[...] = jnp.dot(hs.astype(BF16), wsd_ref[...], preferred_element_type=F32)

    n_rows = tab_ref[0, TAB_META, 0] * RUN_ROWS
    for k in range(TOP_K):
        bp_s[k] = jnp.broadcast_to(bpos_ref[:, k:k + 1], (tt, LANES))
        wq_s[k] = jnp.broadcast_to(w_ref[:, k:k + 1], (tt, LANES))
    for s in range(2):
        @pl.when(i % 2 == s)
        def _():
            _wait_chunks(tab_ref, make_copy_into(s))
            for kb in range(BUF_ROWS // SORT_BLOCK):
                @pl.when(kb * SORT_BLOCK < n_rows)
                def _():
                    qs = []
                    for c0 in range(kb * SORT_BLOCK, (kb + 1) * SORT_BLOCK, LANES):
                        iol = lax.broadcasted_iota(I32, (tt, LANES), 1) + c0
                        q = jnp.zeros((tt, LANES), F32)
                        for k in range(TOP_K):
                            q = jnp.where(iol == bp_s[k], wq_s[k], q)
                        qs.append(q.astype(BF16))
                    rows = _load_packed_rows(buf.at[s], kb * SORT_BLOCK, SORT_BLOCK, sub)
                    acc[...] += jnp.dot(jnp.concatenate(qs, axis=1), rows,
                                        preferred_element_type=F32)

    out_ref[...] = _layernorm(alpha * x1 + acc[...], g2_ref[...], b2_ref[...])


def _combine_call(tab, x1, bpos_tok, w_tok, ys, ws_gu, ws_d, ln2_g, ln2_b, *, alpha):
    t, d = x1.shape
    tt = TILE
    n_tiles = t // tt
    n_exp = tab.shape[2]

    def const(shape):
        return pl.BlockSpec(shape, lambda i: (0,) * len(shape))

    tab_spec = lambda f: pl.BlockSpec((1, 4, n_exp), f, memory_space=pltpu.SMEM)
    return pl.pallas_call(
        functools.partial(_combine_kernel, alpha=alpha),
        grid=(n_tiles,),
        in_specs=[tab_spec(lambda i: (i, 0, 0)),
                  tab_spec(lambda i: (jnp.minimum(i + 1, n_tiles - 1), 0, 0)),
                  pl.BlockSpec((tt, d), lambda i: (i, 0)),
                  pl.BlockSpec((tt, TOP_K), lambda i: (i, 0)),
                  pl.BlockSpec((tt, TOP_K), lambda i: (i, 0)),
                  pl.BlockSpec(memory_space=pl.ANY),
                  const(ws_gu.shape), const(ws_d.shape), const(ln2_g.shape), const(ln2_b.shape)],
        out_specs=pl.BlockSpec((tt, d), lambda i: (i, 0)),
        out_shape=jax.ShapeDtypeStruct((t, d), F32),
        scratch_shapes=[pltpu.VMEM((2, BUF_ROWS * (d // 2 // LANES), LANES), U32),
                        pltpu.VMEM((tt, d), F32),
                        pltpu.VMEM((TOP_K, tt, LANES), I32),
                        pltpu.VMEM((TOP_K, tt, LANES), F32),
                        pltpu.SemaphoreType.DMA((2,))],
        compiler_params=pltpu.CompilerParams(
            dimension_semantics=("arbitrary",), vmem_limit_bytes=VMEM_LIMIT_OTHER),
        name="combine",
    )(tab, tab, x1, bpos_tok, w_tok, ys, ws_gu, ws_d, ln2_g, ln2_b)


def _layer(x, w_in, pool_w, pool_scale, conv_dw, conv_ln_g, conv_ln_b, conv_w_out, w_out,
           ln1_g, ln1_b, w_router, router_bias, w_gate_e, w_up_e, w_down_e,
           ws_gate, ws_up, ws_down, ln2_g, ln2_b, *, alpha):
    b, s, d = x.shape
    t = b * s
    row = lambda v: v.reshape(1, -1)
    x1, logits_t = _mixer_call(
        x, w_in.astype(BF16), pool_w.astype(BF16), row(pool_scale), conv_dw, row(conv_ln_g),
        row(conv_ln_b), conv_w_out.astype(BF16), w_out.astype(BF16), row(ln1_g), row(ln1_b),
        w_router.T.astype(BF16), alpha=alpha)
    w_t, lpos_t, bpos_t, rtab, counts = _route_call(logits_t, router_bias.reshape(-1, 1))

    n = t * TOP_K
    n_blocks = -(-(n + (t // TILE) * N_EXPERTS + N_EXPERTS * (EXPERT_BLOCK - 1 + RUN_ROWS))
                 // EXPERT_BLOCK)
    n_slots = n_blocks * EXPERT_BLOCK
    counts = counts[0].astype(I32)
    n_blk = (counts + RUN_ROWS + EXPERT_BLOCK - 1) // EXPERT_BLOCK
    end_blk = jnp.cumsum(n_blk)
    first_blk = end_blk - n_blk
    pad_start = first_blk * EXPERT_BLOCK
    n_used = end_blk[-1]
    blk = jnp.arange(n_blocks, dtype=I32)
    block_e = jnp.minimum(jnp.sum((end_blk[None, :] <= blk[:, None]).astype(I32), axis=1),
                          N_EXPERTS - 1)
    real_end = (pad_start + counts)[block_e]
    zflag = ((blk >= n_used) | ((blk + 1) * EXPERT_BLOCK > real_end)).astype(I32)
    tail = jnp.stack([n_used, jnp.asarray(n_blocks, I32)])

    rtab = rtab.astype(I32)
    nch = rtab[:, ROUTE_NCH, :]
    meta = jnp.zeros_like(nch).at[:, 0].set(jnp.sum(nch, axis=1)).at[:, 1].set(jnp.max(nch, axis=1))
    slot = rtab[:, ROUTE_BASE, :] + pad_start[None, :]
    tab_d = jnp.stack([slot, nch, rtab[:, ROUTE_START, :], meta], axis=1)
    tab_c = jnp.stack([slot, nch, rtab[:, ROUTE_CHUNK, :], meta], axis=1)

    xs = _dispatch_call(zflag, tab_d, lpos_t, x1, n_slots)
    ys = _expert_call(first_blk, n_blk, tail, xs, w_gate_e, w_up_e, w_down_e)
    ws_gu = jnp.concatenate([ws_gate, ws_up], axis=1).astype(BF16)
    out = _combine_call(tab_c, x1, bpos_t.T, w_t.T, ys, ws_gu, ws_down.astype(BF16), row(ln2_g),
                        row(ln2_b), alpha=alpha)
    return out.reshape(b, s, d)


def kernel(x, w_in, pool_w, pool_scale, conv_dw, conv_ln_g, conv_ln_b, conv_w_out, w_out, ln1_g, ln1_b, w_router, router_bias, w_gate_e, w_up_e, w_down_e, ws_gate, ws_up, ws_down, ln2_g, ln2_b):
    depth = w_in.shape[0]
    alpha = (2.0 * depth) ** 0.25
    for l in range(depth):
        x = _layer(x, w_in[l], pool_w[l], pool_scale[l], conv_dw[l], conv_ln_g[l], conv_ln_b[l],
                   conv_w_out[l], w_out[l], ln1_g[l], ln1_b[l], w_router[l], router_bias[l],
                   w_gate_e[l], w_up_e[l], w_down_e[l], ws_gate[l], ws_up[l], ws_down[l],
                   ln2_g[l], ln2_b[l], alpha=alpha)
    return x
```

```python
import functools

import jax
import jax.numpy as jnp
from jax import lax
from jax.experimental import pallas as pl
from jax.experimental.pallas import tpu as pltpu

F32 = jnp.float32
BF16 = jnp.bfloat16
I32 = jnp.int32
U32 = jnp.uint32

POOL_GROUPS = 4
POOL_WINDOWS = (2, 4, 8, 16)
CONV_KERNEL = 31
N_EXPERTS = 256
TOP_K = 8
N_GROUPS = 8
TOPK_GROUPS = 4
EXPERTS_PER_GROUP = N_EXPERTS // N_GROUPS
ROUTED_SCALE = 2.5
LN_EPS = 1e-5

SUBLANES = 8
LANES = 128
MIX_ROWS = 512
POOL_CHUNK = 32
CONV_CHUNK = 64
CONV_COLS = 256
POOL_HIST = 16
CONV_HIST = 32
TILE = 512
RUN_ROWS = 32
SORT_BLOCK = 512
MAX_CHUNKS = N_EXPERTS + TILE * TOP_K // RUN_ROWS
BUF_ROWS = MAX_CHUNKS * RUN_ROWS
SORT_ROWS = -(-(TILE * TOP_K + N_EXPERTS + RUN_ROWS) // SORT_BLOCK) * SORT_BLOCK
EXPERT_BLOCK = 256
EXPERT_RING = 4
VMEM_LIMIT_MIXER = 56 * 1024 * 1024
VMEM_LIMIT_OTHER = 56 * 1024 * 1024


def _layernorm(z, g, b):
    mu = jnp.mean(z, axis=-1, keepdims=True)
    d = z - mu
    var = jnp.mean(d * d, axis=-1, keepdims=True)
    return d * lax.rsqrt(var + LN_EPS) * g + b


def _silu(v):
    return v * jax.nn.sigmoid(v)


def _store_packed_rows(ref, row0, v):
    n, d = v.shape
    half = d // 2
    sub = half // LANES
    for j in range(sub):
        words = pltpu.pack_elementwise(
            [v[:, j * LANES:(j + 1) * LANES], v[:, half + j * LANES:half + (j + 1) * LANES]],
            packed_dtype=BF16)
        ref[pl.ds(row0 * sub + j, n, stride=sub), :] = words


def _load_packed_rows(ref, row0, n, sub):
    los, his = [], []
    for j in range(sub):
        words = ref[pl.ds(row0 * sub + j, n, stride=sub), :]
        los.append(pltpu.unpack_elementwise(words, index=0, packed_dtype=BF16,
                                            unpacked_dtype=F32).astype(BF16))
        his.append(pltpu.unpack_elementwise(words, index=1, packed_dtype=BF16,
                                            unpacked_dtype=F32).astype(BF16))
    return jnp.concatenate(los + his, axis=1)


def _mixer_kernel(x_ref, w_in_ref, pool_w_ref, pool_scale_ref, dw_ref, cg_ref, cb_ref, cwo_ref,
                  w_out_ref, g1_ref, b1_ref, wr_ref, x1_ref, lt_ref,
                  ubuf, vbuf, rbuf, cvbuf, *, alpha):
    ts = x_ref.shape[1]
    d_model = x_ref.shape[2]
    pw = ubuf.shape[1]
    cw = vbuf.shape[1]
    gi = pw // POOL_GROUPS
    si = pl.program_id(1)

    @pl.when(si == 0)
    def _():
        ubuf[0:POOL_HIST, :] = jnp.zeros((POOL_HIST, pw), F32)
        vbuf[0:CONV_HIST, :] = jnp.zeros((CONV_HIST, cw), F32)

    x = x_ref[0]
    xb = x.astype(BF16)
    ubuf[POOL_HIST:POOL_HIST + ts, :] = jnp.dot(xb, w_in_ref[:, 0:pw], preferred_element_type=F32)
    a = jnp.dot(xb, w_in_ref[:, pw:pw + 2 * cw], preferred_element_type=F32)
    vbuf[CONV_HIST:CONV_HIST + ts, :] = a[:, :cw] * jax.nn.sigmoid(a[:, cw:])
    pos0 = si * ts

    for c in range(ts // POOL_CHUNK):
        r0 = c * POOL_CHUNK
        pos = pos0 + r0 + lax.broadcasted_iota(I32, (POOL_CHUNK, 1), 0)
        for g, w in enumerate(POOL_WINDOWS):
            cols = slice(g * gi, (g + 1) * gi)
            cur = ubuf[pl.ds(POOL_HIST + r0, POOL_CHUNK), cols]
            s = cur
            for j in range(1, w):
                s = s + ubuf[pl.ds(POOL_HIST + r0 - j, POOL_CHUNK), cols]
            cnt = jnp.minimum(pos + 1, w).astype(F32)
            rbuf[pl.ds(r0, POOL_CHUNK), cols] = (s / cnt - cur).astype(BF16)

    for c in range(ts // CONV_CHUNK):
        r0 = c * CONV_CHUNK
        base = CONV_HIST + r0 - (CONV_KERNEL - 1)
        for c0 in range(0, cw, CONV_COLS):
            cols = slice(c0, c0 + CONV_COLS)
            parts = []
            for rho in range(SUBLANES):
                taps = range(rho, CONV_KERNEL, SUBLANES)
                win = vbuf[pl.ds(base + rho, CONV_CHUNK + taps[-1] - rho), cols]
                part = None
                for k in taps:
                    term = dw_ref[k:k + 1, cols] * win[k - rho:k - rho + CONV_CHUNK, :]
                    part = term if part is None else part + term
                parts.append(part)
            cvbuf[pl.ds(r0, CONV_CHUNK), cols] = functools.reduce(lambda p, q: p + q, parts)

    ubuf[0:POOL_HIST, :] = ubuf[ts:ts + POOL_HIST, :]
    vbuf[0:CONV_HIST, :] = vbuf[ts:ts + CONV_HIST, :]

    y_pool = jnp.concatenate(
        [jnp.dot(rbuf[:, g * gi:(g + 1) * gi], pool_w_ref[g], preferred_element_type=F32)
         for g in range(POOL_GROUPS)], axis=1) * pool_scale_ref[...]
    conv = _silu(_layernorm(cvbuf[...], cg_ref[...], cb_ref[...]))
    y_conv = jnp.dot(conv.astype(BF16), cwo_ref[...], preferred_element_type=F32)
    gates = jnp.dot(xb, w_in_ref[:, pw + 2 * cw:], preferred_element_type=F32)
    merged = (jax.nn.sigmoid(gates[:, :d_model]) * y_pool
              + jax.nn.sigmoid(gates[:, d_model:]) * y_conv)
    m = jnp.dot(merged.astype(BF16), w_out_ref[...], preferred_element_type=F32)
    x1 = _layernorm(alpha * x + m, g1_ref[...], b1_ref[...])
    x1_ref[...] = x1
    lt_ref[...] = lax.dot_general(wr_ref[...], x1.astype(BF16), (((1,), (1,)), ((), ())),
                                  preferred_element_type=F32)


def _mixer_call(x, w_in, pool_w, pool_scale, conv_dw, cln_g, cln_b, conv_w_out, w_out,
                ln1_g, ln1_b, w_router_t, *, alpha):
    b, s, d = x.shape
    ts = MIX_ROWS
    ns = s // ts
    pw = pool_w.shape[0] * pool_w.shape[1]
    cw = conv_dw.shape[1]
    n_exp = w_router_t.shape[0]

    def const(shape):
        return pl.BlockSpec(shape, lambda bi, si: (0,) * len(shape))

    return pl.pallas_call(
        functools.partial(_mixer_kernel, alpha=alpha),
        grid=(b, ns),
        in_specs=[
            pl.BlockSpec((1, ts, d), lambda bi, si: (bi, si, 0)),
            const(w_in.shape), const(pool_w.shape), const(pool_scale.shape), const(conv_dw.shape),
            const(cln_g.shape), const(cln_b.shape), const(conv_w_out.shape), const(w_out.shape),
            const(ln1_g.shape), const(ln1_b.shape), const(w_router_t.shape),
        ],
        out_specs=[
            pl.BlockSpec((ts, d), lambda bi, si: (bi * ns + si, 0)),
            pl.BlockSpec((n_exp, ts), lambda bi, si: (0, bi * ns + si)),
        ],
        out_shape=[jax.ShapeDtypeStruct((b * s, d), F32),
                   jax.ShapeDtypeStruct((n_exp, b * s), F32)],
        scratch_shapes=[
            pltpu.VMEM((POOL_HIST + ts, pw), F32),
            pltpu.VMEM((CONV_HIST + ts, cw), F32),
            pltpu.VMEM((ts, pw), BF16),
            pltpu.VMEM((ts, cw), F32),
        ],
        compiler_params=pltpu.CompilerParams(
            dimension_semantics=("arbitrary", "arbitrary"), vmem_limit_bytes=VMEM_LIMIT_MIXER),
        name="mixer",
    )(x, w_in, pool_w, pool_scale, conv_dw, cln_g, cln_b, conv_w_out, w_out, ln1_g, ln1_b,
      w_router_t)


ROUTE_BASE = 0
ROUTE_NCH = 1
ROUTE_CHUNK = 2
ROUTE_START = 3


def _route_kernel(lt_ref, bias_ref, w_ref, lpos_ref, bpos_ref, tab_ref, cnt_ref, carry):
    n_exp, tt = lt_ref.shape
    neg = -jnp.inf

    @pl.when(pl.program_id(0) == 0)
    def _():
        carry[...] = jnp.zeros_like(carry)

    scores = jax.nn.sigmoid(lt_ref[...])
    sel = scores + bias_ref[...]
    sel3 = sel.reshape(N_GROUPS, EXPERTS_PER_GROUP, tt)
    io3 = lax.broadcasted_iota(I32, sel3.shape, 1)
    m1 = jnp.max(sel3, axis=1, keepdims=True)
    i1 = jnp.min(jnp.where(sel3 == m1, io3, EXPERTS_PER_GROUP), axis=1, keepdims=True)
    m2 = jnp.max(jnp.where(io3 == i1, neg, sel3), axis=1, keepdims=True)
    gscore = m1 + m2
    iog = lax.broadcasted_iota(I32, gscore.shape, 0)
    gsel = jnp.zeros(gscore.shape, F32)
    for _ in range(TOPK_GROUPS):
        m = jnp.max(gscore, axis=0, keepdims=True)
        gi = jnp.min(jnp.where(gscore == m, iog, N_GROUPS), axis=0, keepdims=True)
        hit = iog == gi
        gsel = jnp.where(hit, 1.0, gsel)
        gscore = jnp.where(hit, neg, gscore)
    val = jnp.where(gsel > 0.5, sel3, neg).reshape(n_exp, tt)
    ioe = lax.broadcasted_iota(I32, (n_exp, tt), 0)
    member = jnp.zeros((n_exp, tt), F32)
    idxs, ws = [], []
    for _ in range(TOP_K):
        m = jnp.max(val, axis=0, keepdims=True)
        ei = jnp.min(jnp.where(val == m, ioe, n_exp), axis=0, keepdims=True)
        hit = ioe == ei
        idxs.append(ei)
        ws.append(jnp.sum(jnp.where(hit, scores, 0.0), axis=0, keepdims=True))
        member = jnp.where(hit, 1.0, member)
        val = jnp.where(hit, neg, val)
    w = jnp.concatenate(ws, axis=0)
    w_ref[...] = w / jnp.sum(w, axis=0, keepdims=True) * ROUTED_SCALE

    mb = member.astype(BF16)
    before = (lax.broadcasted_iota(I32, (tt, tt), 0) < lax.broadcasted_iota(I32, (tt, tt), 1))
    rank_in_tile = jnp.dot(mb, before.astype(BF16), preferred_element_type=F32)
    c_col = jnp.sum(member, axis=1, keepdims=True)
    c_row = lax.dot_general(jnp.ones((SUBLANES, tt), BF16), mb, (((1,), (1,)), ((), ())),
                            preferred_element_type=F32)

    def even(c):
        return c + (c - 2.0 * jnp.floor(c * 0.5))

    def chunks(c):
        return jnp.maximum(jnp.floor((c + (RUN_ROWS - 1)) * (1.0 / RUN_ROWS)), 1.0)

    ee0 = lax.broadcasted_iota(I32, (n_exp, n_exp), 0)
    ee1 = lax.broadcasted_iota(I32, (n_exp, n_exp), 1)
    lower = (ee1 < ee0).astype(BF16)
    upper = (ee0 < ee1).astype(BF16)

    def prefix_col(v):
        return jnp.dot(lower, jnp.broadcast_to(v, (n_exp, LANES)).astype(BF16),
                       preferred_element_type=F32)[:, 0:1]

    def prefix_row(v):
        return jnp.dot(v.astype(BF16), upper, preferred_element_type=F32)

    chunk_col = prefix_col(chunks(c_col))
    start_col = prefix_col(even(c_col))
    bfull = chunk_col * RUN_ROWS + rank_in_tile
    lfull = start_col + rank_in_tile
    bpos_ref[...] = jnp.concatenate(
        [jnp.sum(jnp.where(ioe == ei, bfull, 0.0), axis=0, keepdims=True) for ei in idxs],
        axis=0).astype(I32)
    lpos_ref[...] = jnp.concatenate(
        [jnp.sum(jnp.where(ioe == ei, lfull, 0.0), axis=0, keepdims=True) for ei in idxs],
        axis=0).astype(I32)
    nch_row = chunks(c_row)
    tab_ref[0] = jnp.concatenate(
        [carry[0:1], nch_row[0:1], prefix_row(nch_row)[0:1], prefix_row(even(c_row))[0:1],
         jnp.zeros((SUBLANES - 4, n_exp), F32)], axis=0)
    carry[...] = carry[...] + even(c_row)
    cnt_ref[...] = carry[...]


def _route_call(logits_t, bias_col):
    n_exp, t = logits_t.shape
    tt = TILE
    return pl.pallas_call(
        _route_kernel,
        grid=(t // tt,),
        in_specs=[pl.BlockSpec((n_exp, tt), lambda i: (0, i)),
                  pl.BlockSpec((n_exp, 1), lambda i: (0, 0))],
        out_specs=[pl.BlockSpec((TOP_K, tt), lambda i: (0, i)),
                   pl.BlockSpec((TOP_K, tt), lambda i: (0, i)),
                   pl.BlockSpec((TOP_K, tt), lambda i: (0, i)),
                   pl.BlockSpec((1, SUBLANES, n_exp), lambda i: (i, 0, 0)),
                   pl.BlockSpec((SUBLANES, n_exp), lambda i: (0, 0))],
        out_shape=[jax.ShapeDtypeStruct((TOP_K, t), F32),
                   jax.ShapeDtypeStruct((TOP_K, t), I32),
                   jax.ShapeDtypeStruct((TOP_K, t), I32),
                   jax.ShapeDtypeStruct((t // tt, SUBLANES, n_exp), F32),
                   jax.ShapeDtypeStruct((SUBLANES, n_exp), F32)],
        scratch_shapes=[pltpu.VMEM((SUBLANES, n_exp), F32)],
        compiler_params=pltpu.CompilerParams(
            dimension_semantics=("arbitrary",), vmem_limit_bytes=VMEM_LIMIT_OTHER),
        name="route",
    )(logits_t, bias_col)


TAB_SLOT = 0
TAB_NCH = 1
TAB_SRC = 2
TAB_META = 3


def _start_chunks(tab_ref, n_exp, src_step, make_copy):
    def first(h, c):
        for u in range(2):
            e = 2 * h + u
            make_copy(tab_ref[0, TAB_SRC, e], tab_ref[0, TAB_SLOT, e]).start(priority=u)
        return c

    lax.fori_loop(0, n_exp // 2, first, 0, unroll=4)

    @pl.when(tab_ref[0, TAB_META, 1] > 1)
    def _():
        def extra(e, c):
            def one(i, c2):
                make_copy(tab_ref[0, TAB_SRC, e] + i * src_step,
                          tab_ref[0, TAB_SLOT, e] + i * RUN_ROWS).start()
                return c2
            lax.fori_loop(1, tab_ref[0, TAB_NCH, e], one, 0)
            return c
        lax.fori_loop(0, n_exp, extra, 0)


def _wait_chunks(tab_ref, make_copy):
    def wait(i, c):
        make_copy(0, 0).wait()
        return c
    lax.fori_loop(0, tab_ref[0, TAB_META, 0], wait, 0)


def _dispatch_kernel(zflag_ref, tab_ref, prev_tab_ref, lpos_ref, x1_ref, xs_hbm,
                     sbuf, zbuf, zsem, sem):
    tt = x1_ref.shape[0]
    n_blocks = zflag_ref.shape[0]
    n_exp = tab_ref.shape[2]
    sub = zbuf.shape[0] // EXPERT_BLOCK
    chunk_rows = RUN_ROWS * sub
    i = pl.program_id(0)

    def zero_copy(blk):
        return pltpu.make_async_copy(
            zbuf, xs_hbm.at[pl.ds(pl.multiple_of(blk * zbuf.shape[0], zbuf.shape[0]),
                                  zbuf.shape[0])], zsem)

    @pl.when(i == 0)
    def _():
        zbuf[...] = jnp.zeros_like(zbuf)

        def start(blk, c):
            @pl.when(zflag_ref[blk] == 1)
            def _():
                zero_copy(blk).start()
            return c

        def wait(blk, c):
            @pl.when(zflag_ref[blk] == 1)
            def _():
                zero_copy(blk).wait()
            return c

        lax.fori_loop(0, n_blocks, start, 0)
        lax.fori_loop(0, n_blocks, wait, 0)

    def make_copy_from(buf):
        def make_copy(row, slot):
            return pltpu.make_async_copy(
                buf.at[pl.ds(pl.multiple_of(row * sub, 2 * sub), chunk_rows)],
                xs_hbm.at[pl.ds(pl.multiple_of(slot * sub, 2 * sub), chunk_rows)], sem)
        return make_copy

    xb = x1_ref[...].astype(BF16)

    def sort_into(buf):
        for rb in range(SORT_ROWS // SORT_BLOCK):
            ior = lax.broadcasted_iota(I32, (SORT_BLOCK, tt), 0) + rb * SORT_BLOCK
            p = jnp.zeros((SORT_BLOCK, tt), F32)
            for k in range(TOP_K):
                p = jnp.where(ior == lpos_ref[k:k + 1, :], 1.0, p)
            _store_packed_rows(buf, rb * SORT_BLOCK,
                               jnp.dot(p.astype(BF16), xb, preferred_element_type=F32))

    for s in range(2):
        @pl.when(i % 2 == s)
        def _():
            sort_into(sbuf.at[s])

    @pl.when(i > 0)
    def _():
        _wait_chunks(prev_tab_ref, make_copy_from(sbuf.at[0]))

    for s in range(2):
        @pl.when(i % 2 == s)
        def _():
            _start_chunks(tab_ref, n_exp, RUN_ROWS, make_copy_from(sbuf.at[s]))

    @pl.when(i == pl.num_programs(0) - 1)
    def _():
        _wait_chunks(tab_ref, make_copy_from(sbuf.at[0]))


def _dispatch_call(zflag, tab, lpos, x1, n_slots):
    t, d = x1.shape
    tt = TILE
    n_exp = tab.shape[2]
    sub = d // 2 // LANES
    tab_spec = lambda f: pl.BlockSpec((1, 4, n_exp), f, memory_space=pltpu.SMEM)
    grid_spec = pltpu.PrefetchScalarGridSpec(
        num_scalar_prefetch=1,
        grid=(t // tt,),
        in_specs=[tab_spec(lambda i, zf: (i, 0, 0)),
                  tab_spec(lambda i, zf: (jnp.maximum(i - 1, 0), 0, 0)),
                  pl.BlockSpec((TOP_K, tt), lambda i, zf: (0, i)),
                  pl.BlockSpec((tt, d), lambda i, zf: (i, 0))],
        out_specs=pl.BlockSpec(memory_space=pl.ANY),
        scratch_shapes=[pltpu.VMEM((2, SORT_ROWS * sub, LANES), U32),
                        pltpu.VMEM((EXPERT_BLOCK * sub, LANES), U32),
                        pltpu.SemaphoreType.DMA(()),
                        pltpu.SemaphoreType.DMA(())],
    )
    return pl.pallas_call(
        _dispatch_kernel,
        grid_spec=grid_spec,
        out_shape=jax.ShapeDtypeStruct((n_slots * sub, LANES), U32),
        compiler_params=pltpu.CompilerParams(
            dimension_semantics=("arbitrary",), vmem_limit_bytes=VMEM_LIMIT_OTHER),
        name="dispatch",
    )(zflag, tab, tab, lpos, x1)


def _expert_kernel(first_ref, nblk_ref, tail_ref, xs_hbm, wg_ref, wu_ref, wd_ref, ys_hbm,
                   wgu_s, wd_s, xbuf, ybuf, isem, osem):
    e = pl.program_id(0)
    last = pl.num_programs(0) - 1
    hid = wg_ref.shape[2]
    blk_rows = xbuf.shape[0] // EXPERT_RING
    sub = blk_rows // EXPERT_BLOCK
    n_used = tail_ref[0]

    def rows_of(blk):
        return pl.ds(pl.multiple_of(blk * blk_rows, blk_rows), blk_rows)

    def in_copy(g):
        s = g % EXPERT_RING
        return pltpu.make_async_copy(xs_hbm.at[rows_of(g)], xbuf.at[rows_of(s)], isem.at[s])

    def out_copy(g):
        s = g % EXPERT_RING
        return pltpu.make_async_copy(ybuf.at[rows_of(s)], ys_hbm.at[rows_of(g)], osem.at[s])

    @pl.when(e == 0)
    def _():
        for g in range(EXPERT_RING - 1):
            in_copy(g).start()

    wgu_s[:, 0:hid] = wg_ref[0].astype(BF16)
    wgu_s[:, hid:2 * hid] = wu_ref[0].astype(BF16)
    wd_s[...] = wd_ref[0].astype(BF16)

    def block(b, c):
        g = first_ref[e] + b
        row0 = (g % EXPERT_RING) * EXPERT_BLOCK
        in_copy(g).wait()

        @pl.when(g + (EXPERT_RING - 1) < n_used)
        def _():
            in_copy(g + (EXPERT_RING - 1)).start()

        @pl.when(g >= EXPERT_RING)
        def _():
            out_copy(g).wait()

        xb = _load_packed_rows(xbuf, row0, EXPERT_BLOCK, sub)
        gu = jnp.dot(xb, wgu_s[...], preferred_element_type=F32)
        h = _silu(gu[:, :hid]) * gu[:, hid:]
        _store_packed_rows(ybuf, row0,
                           jnp.dot(h.astype(BF16), wd_s[...], preferred_element_type=F32))
        out_copy(g).start()
        return c

    lax.fori_loop(0, nblk_ref[e], block, 0)

    @pl.when(e == last)
    def _():
        for g in range(EXPERT_RING):
            out_copy(g).wait()
        ybuf[0:blk_rows, :] = jnp.zeros((blk_rows, ybuf.shape[1]), ybuf.dtype)

        def start(b, c):
            pltpu.make_async_copy(ybuf.at[rows_of(0)], ys_hbm.at[rows_of(b)], osem.at[0]).start()
            return c

        def wait(b, c):
            pltpu.make_async_copy(ybuf.at[rows_of(0)], ys_hbm.at[rows_of(b)], osem.at[0]).wait()
            return c

        lax.fori_loop(tail_ref[0], tail_ref[1], start, 0)
        lax.fori_loop(tail_ref[0], tail_ref[1], wait, 0)


def _expert_call(first_blk, n_blk, tail, xs, w_gate_e, w_up_e, w_down_e):
    n_exp, d, hid = w_gate_e.shape
    blk_rows = EXPERT_BLOCK * (d // 2 // LANES)
    grid_spec = pltpu.PrefetchScalarGridSpec(
        num_scalar_prefetch=3,
        grid=(n_exp,),
        in_specs=[pl.BlockSpec(memory_space=pl.ANY),
                  pl.BlockSpec((1, d, hid), lambda e, *_: (e, 0, 0)),
                  pl.BlockSpec((1, d, hid), lambda e, *_: (e, 0, 0)),
                  pl.BlockSpec((1, hid, d), lambda e, *_: (e, 0, 0))],
        out_specs=pl.BlockSpec(memory_space=pl.ANY),
        scratch_shapes=[pltpu.VMEM((d, 2 * hid), BF16), pltpu.VMEM((hid, d), BF16),
                        pltpu.VMEM((EXPERT_RING * blk_rows, LANES), U32),
                        pltpu.VMEM((EXPERT_RING * blk_rows, LANES), U32),
                        pltpu.SemaphoreType.DMA((EXPERT_RING,)),
                        pltpu.SemaphoreType.DMA((EXPERT_RING,))],
    )
    return pl.pallas_call(
        _expert_kernel,
        grid_spec=grid_spec,
        out_shape=jax.ShapeDtypeStruct(xs.shape, U32),
        compiler_params=pltpu.CompilerParams(
            dimension_semantics=("arbitrary",), vmem_limit_bytes=VMEM_LIMIT_OTHER),
        name="experts",
    )(first_blk, n_blk, tail, xs, w_gate_e, w_up_e, w_down_e)


def _combine_kernel(tab_ref, x1_ref, bpos_ref, w_ref, ys_hbm, wsgu_ref, wsd_ref, g2_ref, b2_ref,
                    out_ref, buf, acc, bp_s, wq_s, sem, *, alpha):
    tt = x1_ref.shape[0]
    hid = wsd_ref.shape[0]
    sub = buf.shape[0] // BUF_ROWS
    chunk_rows = RUN_ROWS * sub
    n_exp = tab_ref.shape[2]
    cpb = SORT_BLOCK // RUN_ROWS

    def make_copy(chunk, slot):
        return pltpu.make_async_copy(
            ys_hbm.at[pl.ds(pl.multiple_of(slot * sub, 2 * sub), chunk_rows)],
            buf.at[pl.ds(pl.multiple_of(chunk * chunk_rows, chunk_rows), chunk_rows)],
            sem.at[chunk // cpb])

    @pl.when(pl.program_id(0) == 0)
    def _():
        buf[...] = jnp.zeros_like(buf)

    _start_chunks(tab_ref, n_exp, 1, make_copy)

    x1 = x1_ref[...]
    gu = jnp.dot(x1.astype(BF16), wsgu_ref[...], preferred_element_type=F32)
    hs = _silu(gu[:, :hid]) * gu[:, hid:]
    acc[...] = jnp.dot(hs.astype(BF16), wsd_ref[...], preferred_element_type=F32)

    for k in range(TOP_K):
        bp_s[k] = jnp.broadcast_to(bpos_ref[:, k:k + 1], (tt, LANES))
        wq_s[k] = jnp.broadcast_to(w_ref[:, k:k + 1], (tt, LANES))

    n_ch = tab_ref[0, TAB_META, 0]
    for kb in range(BUF_ROWS // SORT_BLOCK):
        @pl.when(kb * cpb < n_ch)
        def _():
            def wait(j, c):
                make_copy(kb * cpb, 0).wait()
                return c
            lax.fori_loop(0, jnp.minimum(n_ch - kb * cpb, cpb), wait, 0)
            qs = []
            for c0 in range(kb * SORT_BLOCK, (kb + 1) * SORT_BLOCK, LANES):
                iol = lax.broadcasted_iota(I32, (tt, LANES), 1) + c0
                q = jnp.zeros((tt, LANES), F32)
                for k in range(TOP_K):
                    q = jnp.where(iol == bp_s[k], wq_s[k], q)
                qs.append(q.astype(BF16))
            rows = _load_packed_rows(buf, kb * SORT_BLOCK, SORT_BLOCK, sub)
            acc[...] += jnp.dot(jnp.concatenate(qs, axis=1), rows, preferred_element_type=F32)

    out_ref[...] = _layernorm(alpha * x1 + acc[...], g2_ref[...], b2_ref[...])


def _combine_call(tab, x1, bpos_tok, w_tok, ys, ws_gu, ws_d, ln2_g, ln2_b, *, alpha):
    t, d = x1.shape
    tt = TILE
    n_tiles = t // tt
    n_exp = tab.shape[2]

    def const(shape):
        return pl.BlockSpec(shape, lambda i: (0,) * len(shape))

    tab_spec = lambda f: pl.BlockSpec((1, 4, n_exp), f, memory_space=pltpu.SMEM)
    return pl.pallas_call(
        functools.partial(_combine_kernel, alpha=alpha),
        grid=(n_tiles,),
        in_specs=[tab_spec(lambda i: (i, 0, 0)),
                  pl.BlockSpec((tt, d), lambda i: (i, 0)),
                  pl.BlockSpec((tt, TOP_K), lambda i: (i, 0)),
                  pl.BlockSpec((tt, TOP_K), lambda i: (i, 0)),
                  pl.BlockSpec(memory_space=pl.ANY),
                  const(ws_gu.shape), const(ws_d.shape), const(ln2_g.shape), const(ln2_b.shape)],
        out_specs=pl.BlockSpec((tt, d), lambda i: (i, 0)),
        out_shape=jax.ShapeDtypeStruct((t, d), F32),
        scratch_shapes=[pltpu.VMEM((BUF_ROWS * (d // 2 // LANES), LANES), U32),
                        pltpu.VMEM((tt, d), F32),
                        pltpu.VMEM((TOP_K, tt, LANES), I32),
                        pltpu.VMEM((TOP_K, tt, LANES), F32),
                        pltpu.SemaphoreType.DMA((BUF_ROWS // SORT_BLOCK,))],
        compiler_params=pltpu.CompilerParams(
            dimension_semantics=("arbitrary",), vmem_limit_bytes=VMEM_LIMIT_OTHER),
        name="combine",
    )(tab, x1, bpos_tok, w_tok, ys, ws_gu, ws_d, ln2_g, ln2_b)


def _layer(x, w_in, pool_w, pool_scale, conv_dw, conv_ln_g, conv_ln_b, conv_w_out, w_out,
           ln1_g, ln1_b, w_router, router_bias, w_gate_e, w_up_e, w_down_e,
           ws_gate, ws_up, ws_down, ln2_g, ln2_b, *, alpha):
    b, s, d = x.shape
    t = b * s
    row = lambda v: v.reshape(1, -1)
    x1, logits_t = _mixer_call(
        x, w_in.astype(BF16), pool_w.astype(BF16), row(pool_scale), conv_dw, row(conv_ln_g),
        row(conv_ln_b), conv_w_out.astype(BF16), w_out.astype(BF16), row(ln1_g), row(ln1_b),
        w_router.T.astype(BF16), alpha=alpha)
    w_t, lpos_t, bpos_t, rtab, counts = _route_call(logits_t, router_bias.reshape(-1, 1))

    n = t * TOP_K
    n_blocks = -(-(n + (t // TILE) * N_EXPERTS + N_EXPERTS * (EXPERT_BLOCK - 1 + RUN_ROWS))
                 // EXPERT_BLOCK)
    n_slots = n_blocks * EXPERT_BLOCK
    counts = counts[0].astype(I32)
    n_blk = (counts + RUN_ROWS + EXPERT_BLOCK - 1) // EXPERT_BLOCK
    end_blk = jnp.cumsum(n_blk)
    first_blk = end_blk - n_blk
    pad_start = first_blk * EXPERT_BLOCK
    n_used = end_blk[-1]
    blk = jnp.arange(n_blocks, dtype=I32)
    block_e = jnp.minimum(jnp.sum((end_blk[None, :] <= blk[:, None]).astype(I32), axis=1),
                          N_EXPERTS - 1)
    real_end = (pad_start + counts)[block_e]
    zflag = ((blk >= n_used) | ((blk + 1) * EXPERT_BLOCK > real_end)).astype(I32)
    tail = jnp.stack([n_used, jnp.asarray(n_blocks, I32)])

    rtab = rtab.astype(I32)
    nch = rtab[:, ROUTE_NCH, :]
    meta = jnp.zeros_like(nch).at[:, 0].set(jnp.sum(nch, axis=1)).at[:, 1].set(jnp.max(nch, axis=1))
    slot = rtab[:, ROUTE_BASE, :] + pad_start[None, :]
    tab_d = jnp.stack([slot, nch, rtab[:, ROUTE_START, :], meta], axis=1)
    tab_c = jnp.stack([slot, nch, rtab[:, ROUTE_CHUNK, :], meta], axis=1)

    xs = _dispatch_call(zflag, tab_d, lpos_t, x1, n_slots)
    ys = _expert_call(first_blk, n_blk, tail, xs, w_gate_e, w_up_e, w_down_e)
    ws_gu = jnp.concatenate([ws_gate, ws_up], axis=1).astype(BF16)
    out = _combine_call(tab_c, x1, bpos_t.T, w_t.T, ys, ws_gu, ws_down.astype(BF16), row(ln2_g),
                        row(ln2_b), alpha=alpha)
    return out.reshape(b, s, d)


def kernel(x, w_in, pool_w, pool_scale, conv_dw, conv_ln_g, conv_ln_b, conv_w_out, w_out, ln1_g, ln1_b, w_router, router_bias, w_gate_e, w_up_e, w_down_e, ws_gate, ws_up, ws_down, ln2_g, ln2_b):
    depth = w_in.shape[0]
    alpha = (2.0 * depth) ** 0.25
    for l in range(depth):
        x = _layer(x, w_in[l], pool_w[l], pool_scale[l], conv_dw[l], conv_ln_g[l], conv_ln_b[l],
                   conv_w_out[l], w_out[l], ln1_g[l], ln1_b[l], w_router[l], router_bias[l],
                   w_gate_e[l], w_up_e[l], w_down_e[l], ws_gate[l], ws_up[l], ws_down[l],
                   ln2_g[l], ln2_b[l], alpha=alpha)
    return x
```

```python
import functools

import jax
import jax.numpy as jnp
from jax import lax
from jax.experimental import pallas as pl
from jax.experimental.pallas import tpu as pltpu

F32 = jnp.float32
BF16 = jnp.bfloat16
I32 = jnp.int32
U32 = jnp.uint32
I16 = jnp.int16

POOL_GROUPS = 4
POOL_WINDOWS = (2, 4, 8, 16)
CONV_KERNEL = 31
N_EXPERTS = 256
TOP_K = 8
N_GROUPS = 8
TOPK_GROUPS = 4
EXPERTS_PER_GROUP = N_EXPERTS // N_GROUPS
ROUTED_SCALE = 2.5
LN_EPS = 1e-5

SUBLANES = 8
LANES = 128
MIX_ROWS = 512
POOL_CHUNK = 32
CONV_CHUNK = 64
CONV_COLS = 256
POOL_HIST = 16
CONV_HIST = 32
TILE = 512
RUN_ROWS = 32
SORT_BLOCK = 512
MAX_CHUNKS = N_EXPERTS + TILE * TOP_K // RUN_ROWS
BUF_ROWS = MAX_CHUNKS * RUN_ROWS
SORT_ROWS = -(-(TILE * TOP_K + N_EXPERTS + RUN_ROWS) // SORT_BLOCK) * SORT_BLOCK
EXPERT_BLOCK = 256
EXPERT_RING = 4
VMEM_LIMIT_MIXER = 56 * 1024 * 1024
VMEM_LIMIT_OTHER = 56 * 1024 * 1024


def _layernorm(z, g, b):
    mu = jnp.mean(z, axis=-1, keepdims=True)
    d = z - mu
    var = jnp.mean(d * d, axis=-1, keepdims=True)
    return d * lax.rsqrt(var + LN_EPS) * g + b


def _silu(v):
    return v * jax.nn.sigmoid(v)


def _store_packed_rows(ref, row0, v):
    n, d = v.shape
    half = d // 2
    sub = half // LANES
    for j in range(sub):
        words = pltpu.pack_elementwise(
            [v[:, j * LANES:(j + 1) * LANES], v[:, half + j * LANES:half + (j + 1) * LANES]],
            packed_dtype=BF16)
        ref[pl.ds(row0 * sub + j, n, stride=sub), :] = words


def _load_packed_rows(ref, row0, n, sub):
    los, his = [], []
    for j in range(sub):
        words = ref[pl.ds(row0 * sub + j, n, stride=sub), :]
        los.append(pltpu.unpack_elementwise(words, index=0, packed_dtype=BF16,
                                            unpacked_dtype=F32).astype(BF16))
        his.append(pltpu.unpack_elementwise(words, index=1, packed_dtype=BF16,
                                            unpacked_dtype=F32).astype(BF16))
    return jnp.concatenate(los + his, axis=1)


def _mixer_kernel(x_ref, w_in_ref, pool_w_ref, pool_scale_ref, dw_ref, cg_ref, cb_ref, cwo_ref,
                  w_out_ref, g1_ref, b1_ref, wr_ref, x1_ref, lt_ref,
                  ubuf, vbuf, rbuf, cvbuf, *, alpha):
    ts = x_ref.shape[1]
    d_model = x_ref.shape[2]
    pw = ubuf.shape[1]
    cw = vbuf.shape[1]
    gi = pw // POOL_GROUPS
    si = pl.program_id(1)

    @pl.when(si == 0)
    def _():
        ubuf[0:POOL_HIST, :] = jnp.zeros((POOL_HIST, pw), F32)
        vbuf[0:CONV_HIST, :] = jnp.zeros((CONV_HIST, cw), F32)

    x = x_ref[0]
    xb = x.astype(BF16)
    ubuf[POOL_HIST:POOL_HIST + ts, :] = jnp.dot(xb, w_in_ref[:, 0:pw], preferred_element_type=F32)
    a = jnp.dot(xb, w_in_ref[:, pw:pw + 2 * cw], preferred_element_type=F32)
    vbuf[CONV_HIST:CONV_HIST + ts, :] = a[:, :cw] * jax.nn.sigmoid(a[:, cw:])
    pos0 = si * ts

    for c in range(ts // POOL_CHUNK):
        r0 = c * POOL_CHUNK
        pos = pos0 + r0 + lax.broadcasted_iota(I32, (POOL_CHUNK, 1), 0)
        for g, w in enumerate(POOL_WINDOWS):
            cols = slice(g * gi, (g + 1) * gi)
            cur = ubuf[pl.ds(POOL_HIST + r0, POOL_CHUNK), cols]
            s = cur
            for j in range(1, w):
                s = s + ubuf[pl.ds(POOL_HIST + r0 - j, POOL_CHUNK), cols]
            cnt = jnp.minimum(pos + 1, w).astype(F32)
            rbuf[pl.ds(r0, POOL_CHUNK), cols] = (s / cnt - cur).astype(BF16)

    for c in range(ts // CONV_CHUNK):
        r0 = c * CONV_CHUNK
        base = CONV_HIST + r0 - (CONV_KERNEL - 1)
        for c0 in range(0, cw, CONV_COLS):
            cols = slice(c0, c0 + CONV_COLS)
            parts = []
            for rho in range(SUBLANES):
                taps = range(rho, CONV_KERNEL, SUBLANES)
                win = vbuf[pl.ds(base + rho, CONV_CHUNK + taps[-1] - rho), cols]
                part = None
                for k in taps:
                    term = dw_ref[k:k + 1, cols] * win[k - rho:k - rho + CONV_CHUNK, :]
                    part = term if part is None else part + term
                parts.append(part)
            cvbuf[pl.ds(r0, CONV_CHUNK), cols] = functools.reduce(lambda p, q: p + q, parts)

    ubuf[0:POOL_HIST, :] = ubuf[ts:ts + POOL_HIST, :]
    vbuf[0:CONV_HIST, :] = vbuf[ts:ts + CONV_HIST, :]

    y_pool = jnp.concatenate(
        [jnp.dot(rbuf[:, g * gi:(g + 1) * gi], pool_w_ref[g], preferred_element_type=F32)
         for g in range(POOL_GROUPS)], axis=1) * pool_scale_ref[...]
    conv = _silu(_layernorm(cvbuf[...], cg_ref[...], cb_ref[...]))
    y_conv = jnp.dot(conv.astype(BF16), cwo_ref[...], preferred_element_type=F32)
    gates = jnp.dot(xb, w_in_ref[:, pw + 2 * cw:], preferred_element_type=F32)
    merged = (jax.nn.sigmoid(gates[:, :d_model]) * y_pool
              + jax.nn.sigmoid(gates[:, d_model:]) * y_conv)
    m = jnp.dot(merged.astype(BF16), w_out_ref[...], preferred_element_type=F32)
    x1 = _layernorm(alpha * x + m, g1_ref[...], b1_ref[...])
    x1_ref[...] = x1
    lt_ref[...] = lax.dot_general(wr_ref[...], x1.astype(BF16), (((1,), (1,)), ((), ())),
                                  preferred_element_type=F32)


def _mixer_call(x, w_in, pool_w, pool_scale, conv_dw, cln_g, cln_b, conv_w_out, w_out,
                ln1_g, ln1_b, w_router_t, *, alpha):
    b, s, d = x.shape
    ts = MIX_ROWS
    ns = s // ts
    pw = pool_w.shape[0] * pool_w.shape[1]
    cw = conv_dw.shape[1]
    n_exp = w_router_t.shape[0]

    def const(shape):
        return pl.BlockSpec(shape, lambda bi, si: (0,) * len(shape))

    return pl.pallas_call(
        functools.partial(_mixer_kernel, alpha=alpha),
        grid=(b, ns),
        in_specs=[
            pl.BlockSpec((1, ts, d), lambda bi, si: (bi, si, 0)),
            const(w_in.shape), const(pool_w.shape), const(pool_scale.shape), const(conv_dw.shape),
            const(cln_g.shape), const(cln_b.shape), const(conv_w_out.shape), const(w_out.shape),
            const(ln1_g.shape), const(ln1_b.shape), const(w_router_t.shape),
        ],
        out_specs=[
            pl.BlockSpec((ts, d), lambda bi, si: (bi * ns + si, 0)),
            pl.BlockSpec((n_exp, ts), lambda bi, si: (0, bi * ns + si)),
        ],
        out_shape=[jax.ShapeDtypeStruct((b * s, d), F32),
                   jax.ShapeDtypeStruct((n_exp, b * s), F32)],
        scratch_shapes=[
            pltpu.VMEM((POOL_HIST + ts, pw), F32),
            pltpu.VMEM((CONV_HIST + ts, cw), F32),
            pltpu.VMEM((ts, pw), BF16),
            pltpu.VMEM((ts, cw), F32),
        ],
        compiler_params=pltpu.CompilerParams(
            dimension_semantics=("arbitrary", "arbitrary"), vmem_limit_bytes=VMEM_LIMIT_MIXER),
        name="mixer",
    )(x, w_in, pool_w, pool_scale, conv_dw, cln_g, cln_b, conv_w_out, w_out, ln1_g, ln1_b,
      w_router_t)


ROUTE_BASE = 0
ROUTE_NCH = 1
ROUTE_CHUNK = 2
ROUTE_START = 3


def _route_kernel(lt_ref, bias_ref, w_ref, lpos_ref, bpos_ref, tab_ref, cnt_ref, carry):
    n_exp, tt = lt_ref.shape
    neg = -jnp.inf

    @pl.when(pl.program_id(0) == 0)
    def _():
        carry[...] = jnp.zeros_like(carry)

    scores = jax.nn.sigmoid(lt_ref[...])
    sel = scores + bias_ref[...]
    sel3 = sel.reshape(N_GROUPS, EXPERTS_PER_GROUP, tt)
    io3 = lax.broadcasted_iota(I32, sel3.shape, 1)
    m1 = jnp.max(sel3, axis=1, keepdims=True)
    i1 = jnp.min(jnp.where(sel3 == m1, io3, EXPERTS_PER_GROUP), axis=1, keepdims=True)
    m2 = jnp.max(jnp.where(io3 == i1, neg, sel3), axis=1, keepdims=True)
    gscore = m1 + m2
    iog = lax.broadcasted_iota(I32, gscore.shape, 0)
    gsel = jnp.zeros(gscore.shape, F32)
    for _ in range(TOPK_GROUPS):
        m = jnp.max(gscore, axis=0, keepdims=True)
        gi = jnp.min(jnp.where(gscore == m, iog, N_GROUPS), axis=0, keepdims=True)
        hit = iog == gi
        gsel = jnp.where(hit, 1.0, gsel)
        gscore = jnp.where(hit, neg, gscore)
    val = jnp.where(gsel > 0.5, sel3, neg).reshape(n_exp, tt)
    ioe = lax.broadcasted_iota(I32, (n_exp, tt), 0)
    member = jnp.zeros((n_exp, tt), F32)
    idxs, ws = [], []
    for _ in range(TOP_K):
        m = jnp.max(val, axis=0, keepdims=True)
        ei = jnp.min(jnp.where(val == m, ioe, n_exp), axis=0, keepdims=True)
        hit = ioe == ei
        idxs.append(ei)
        ws.append(jnp.sum(jnp.where(hit, scores, 0.0), axis=0, keepdims=True))
        member = jnp.where(hit, 1.0, member)
        val = jnp.where(hit, neg, val)
    w = jnp.concatenate(ws, axis=0)
    w_ref[...] = w / jnp.sum(w, axis=0, keepdims=True) * ROUTED_SCALE

    mb = member.astype(BF16)
    before = (lax.broadcasted_iota(I32, (tt, tt), 0) < lax.broadcasted_iota(I32, (tt, tt), 1))
    rank_in_tile = jnp.dot(mb, before.astype(BF16), preferred_element_type=F32)
    c_col = jnp.sum(member, axis=1, keepdims=True)
    c_row = lax.dot_general(jnp.ones((SUBLANES, tt), BF16), mb, (((1,), (1,)), ((), ())),
                            preferred_element_type=F32)

    def even(c):
        return c + (c - 2.0 * jnp.floor(c * 0.5))

    def chunks(c):
        return jnp.maximum(jnp.floor((c + (RUN_ROWS - 1)) * (1.0 / RUN_ROWS)), 1.0)

    ee0 = lax.broadcasted_iota(I32, (n_exp, n_exp), 0)
    ee1 = lax.broadcasted_iota(I32, (n_exp, n_exp), 1)
    lower = (ee1 < ee0).astype(BF16)
    upper = (ee0 < ee1).astype(BF16)

    def prefix_col(v):
        return jnp.dot(lower, jnp.broadcast_to(v, (n_exp, LANES)).astype(BF16),
                       preferred_element_type=F32)[:, 0:1]

    def prefix_row(v):
        return jnp.dot(v.astype(BF16), upper, preferred_element_type=F32)

    chunk_col = prefix_col(chunks(c_col))
    start_col = prefix_col(even(c_col))
    bfull = chunk_col * RUN_ROWS + rank_in_tile
    lfull = start_col + rank_in_tile
    bpos_ref[...] = jnp.concatenate(
        [jnp.sum(jnp.where(ioe == ei, bfull, 0.0), axis=0, keepdims=True) for ei in idxs],
        axis=0).astype(I32)
    lpos_ref[...] = jnp.concatenate(
        [jnp.sum(jnp.where(ioe == ei, lfull, 0.0), axis=0, keepdims=True) for ei in idxs],
        axis=0).astype(I32)
    nch_row = chunks(c_row)
    tab_ref[0] = jnp.concatenate(
        [carry[0:1], nch_row[0:1], prefix_row(nch_row)[0:1], prefix_row(even(c_row))[0:1],
         jnp.zeros((SUBLANES - 4, n_exp), F32)], axis=0)
    carry[...] = carry[...] + even(c_row)
    cnt_ref[...] = carry[...]


def _route_call(logits_t, bias_col):
    n_exp, t = logits_t.shape
    tt = TILE
    return pl.pallas_call(
        _route_kernel,
        grid=(t // tt,),
        in_specs=[pl.BlockSpec((n_exp, tt), lambda i: (0, i)),
                  pl.BlockSpec((n_exp, 1), lambda i: (0, 0))],
        out_specs=[pl.BlockSpec((TOP_K, tt), lambda i: (0, i)),
                   pl.BlockSpec((TOP_K, tt), lambda i: (0, i)),
                   pl.BlockSpec((TOP_K, tt), lambda i: (0, i)),
                   pl.BlockSpec((1, SUBLANES, n_exp), lambda i: (i, 0, 0)),
                   pl.BlockSpec((SUBLANES, n_exp), lambda i: (0, 0))],
        out_shape=[jax.ShapeDtypeStruct((TOP_K, t), F32),
                   jax.ShapeDtypeStruct((TOP_K, t), I32),
                   jax.ShapeDtypeStruct((TOP_K, t), I32),
                   jax.ShapeDtypeStruct((t // tt, SUBLANES, n_exp), F32),
                   jax.ShapeDtypeStruct((SUBLANES, n_exp), F32)],
        scratch_shapes=[pltpu.VMEM((SUBLANES, n_exp), F32)],
        compiler_params=pltpu.CompilerParams(
            dimension_semantics=("arbitrary",), vmem_limit_bytes=VMEM_LIMIT_OTHER),
        name="route",
    )(logits_t, bias_col)


TAB_SLOT = 0
TAB_NCH = 1
TAB_SRC = 2
TAB_META = 3


def _start_chunks(tab_ref, n_exp, src_step, make_copy):
    def first(h, c):
        for u in range(2):
            e = 2 * h + u
            make_copy(tab_ref[0, TAB_SRC, e], tab_ref[0, TAB_SLOT, e]).start(priority=u)
        return c

    lax.fori_loop(0, n_exp // 2, first, 0, unroll=4)

    @pl.when(tab_ref[0, TAB_META, 1] > 1)
    def _():
        def extra(e, c):
            def one(i, c2):
                make_copy(tab_ref[0, TAB_SRC, e] + i * src_step,
                          tab_ref[0, TAB_SLOT, e] + i * RUN_ROWS).start()
                return c2
            lax.fori_loop(1, tab_ref[0, TAB_NCH, e], one, 0)
            return c
        lax.fori_loop(0, n_exp, extra, 0)


def _wait_chunks(tab_ref, make_copy):
    def wait(i, c):
        make_copy(0, 0).wait()
        return c
    lax.fori_loop(0, tab_ref[0, TAB_META, 0], wait, 0)


def _dispatch_kernel(zflag_ref, tab_ref, prev_tab_ref, lpos_ref, x1_ref, xs_hbm,
                     sbuf, zbuf, zsem, sem):
    tt = x1_ref.shape[0]
    n_blocks = zflag_ref.shape[0]
    n_exp = tab_ref.shape[2]
    sub = zbuf.shape[0] // EXPERT_BLOCK
    chunk_rows = RUN_ROWS * sub
    i = pl.program_id(0)

    def zero_copy(blk):
        return pltpu.make_async_copy(
            zbuf, xs_hbm.at[pl.ds(pl.multiple_of(blk * zbuf.shape[0], zbuf.shape[0]),
                                  zbuf.shape[0])], zsem)

    @pl.when(i == 0)
    def _():
        zbuf[...] = jnp.zeros_like(zbuf)

        def start(blk, c):
            @pl.when(zflag_ref[blk] == 1)
            def _():
                zero_copy(blk).start()
            return c

        def wait(blk, c):
            @pl.when(zflag_ref[blk] == 1)
            def _():
                zero_copy(blk).wait()
            return c

        lax.fori_loop(0, n_blocks, start, 0)
        lax.fori_loop(0, n_blocks, wait, 0)

    def make_copy_from(buf):
        def make_copy(row, slot):
            return pltpu.make_async_copy(
                buf.at[pl.ds(pl.multiple_of(row * sub, 2 * sub), chunk_rows)],
                xs_hbm.at[pl.ds(pl.multiple_of(slot * sub, 2 * sub), chunk_rows)], sem)
        return make_copy

    xb = x1_ref[...].astype(BF16)

    def sort_into(buf):
        for rb in range(SORT_ROWS // SORT_BLOCK):
            ior = lax.broadcasted_iota(I32, (SORT_BLOCK, tt), 0) + rb * SORT_BLOCK
            p = jnp.zeros((SORT_BLOCK, tt), F32)
            for k in range(TOP_K):
                p = jnp.where(ior == lpos_ref[k:k + 1, :], 1.0, p)
            _store_packed_rows(buf, rb * SORT_BLOCK,
                               jnp.dot(p.astype(BF16), xb, preferred_element_type=F32))

    for s in range(2):
        @pl.when(i % 2 == s)
        def _():
            sort_into(sbuf.at[s])

    @pl.when(i > 0)
    def _():
        _wait_chunks(prev_tab_ref, make_copy_from(sbuf.at[0]))

    for s in range(2):
        @pl.when(i % 2 == s)
        def _():
            _start_chunks(tab_ref, n_exp, RUN_ROWS, make_copy_from(sbuf.at[s]))

    @pl.when(i == pl.num_programs(0) - 1)
    def _():
        _wait_chunks(tab_ref, make_copy_from(sbuf.at[0]))


def _dispatch_call(zflag, tab, lpos, x1, n_slots):
    t, d = x1.shape
    tt = TILE
    n_exp = tab.shape[2]
    sub = d // 2 // LANES
    tab_spec = lambda f: pl.BlockSpec((1, 4, n_exp), f, memory_space=pltpu.SMEM)
    grid_spec = pltpu.PrefetchScalarGridSpec(
        num_scalar_prefetch=1,
        grid=(t // tt,),
        in_specs=[tab_spec(lambda i, zf: (i, 0, 0)),
                  tab_spec(lambda i, zf: (jnp.maximum(i - 1, 0), 0, 0)),
                  pl.BlockSpec((TOP_K, tt), lambda i, zf: (0, i)),
                  pl.BlockSpec((tt, d), lambda i, zf: (i, 0))],
        out_specs=pl.BlockSpec(memory_space=pl.ANY),
        scratch_shapes=[pltpu.VMEM((2, SORT_ROWS * sub, LANES), U32),
                        pltpu.VMEM((EXPERT_BLOCK * sub, LANES), U32),
                        pltpu.SemaphoreType.DMA(()),
                        pltpu.SemaphoreType.DMA(())],
    )
    return pl.pallas_call(
        _dispatch_kernel,
        grid_spec=grid_spec,
        out_shape=jax.ShapeDtypeStruct((n_slots * sub, LANES), U32),
        compiler_params=pltpu.CompilerParams(
            dimension_semantics=("arbitrary",), vmem_limit_bytes=VMEM_LIMIT_OTHER),
        name="dispatch",
    )(zflag, tab, tab, lpos, x1)


def _expert_kernel(first_ref, nblk_ref, tail_ref, xs_hbm, wg_ref, wu_ref, wd_ref, ys_hbm,
                   wgu_s, wd_s, xbuf, ybuf, isem, osem):
    e = pl.program_id(0)
    last = pl.num_programs(0) - 1
    hid = wg_ref.shape[2]
    blk_rows = xbuf.shape[0] // EXPERT_RING
    sub = blk_rows // EXPERT_BLOCK
    n_used = tail_ref[0]

    def rows_of(blk):
        return pl.ds(pl.multiple_of(blk * blk_rows, blk_rows), blk_rows)

    def in_copy(g):
        s = g % EXPERT_RING
        return pltpu.make_async_copy(xs_hbm.at[rows_of(g)], xbuf.at[rows_of(s)], isem.at[s])

    def out_copy(g):
        s = g % EXPERT_RING
        return pltpu.make_async_copy(ybuf.at[rows_of(s)], ys_hbm.at[rows_of(g)], osem.at[s])

    @pl.when(e == 0)
    def _():
        for g in range(EXPERT_RING - 1):
            in_copy(g).start()

    wgu_s[:, 0:hid] = wg_ref[0].astype(BF16)
    wgu_s[:, hid:2 * hid] = wu_ref[0].astype(BF16)
    wd_s[...] = wd_ref[0].astype(BF16)

    def block(b, c):
        g = first_ref[e] + b
        row0 = (g % EXPERT_RING) * EXPERT_BLOCK
        in_copy(g).wait()

        @pl.when(g + (EXPERT_RING - 1) < n_used)
        def _():
            in_copy(g + (EXPERT_RING - 1)).start()

        @pl.when(g >= EXPERT_RING)
        def _():
            out_copy(g).wait()

        xb = _load_packed_rows(xbuf, row0, EXPERT_BLOCK, sub)
        gu = jnp.dot(xb, wgu_s[...], preferred_element_type=F32)
        h = _silu(gu[:, :hid]) * gu[:, hid:]
        _store_packed_rows(ybuf, row0,
                           jnp.dot(h.astype(BF16), wd_s[...], preferred_element_type=F32))
        out_copy(g).start()
        return c

    lax.fori_loop(0, nblk_ref[e], block, 0)

    @pl.when(e == last)
    def _():
        for g in range(EXPERT_RING):
            out_copy(g).wait()
        ybuf[0:blk_rows, :] = jnp.zeros((blk_rows, ybuf.shape[1]), ybuf.dtype)

        def start(b, c):
            pltpu.make_async_copy(ybuf.at[rows_of(0)], ys_hbm.at[rows_of(b)], osem.at[0]).start()
            return c

        def wait(b, c):
            pltpu.make_async_copy(ybuf.at[rows_of(0)], ys_hbm.at[rows_of(b)], osem.at[0]).wait()
            return c

        lax.fori_loop(tail_ref[0], tail_ref[1], start, 0)
        lax.fori_loop(tail_ref[0], tail_ref[1], wait, 0)


def _expert_call(first_blk, n_blk, tail, xs, w_gate_e, w_up_e, w_down_e):
    n_exp, d, hid = w_gate_e.shape
    blk_rows = EXPERT_BLOCK * (d // 2 // LANES)
    grid_spec = pltpu.PrefetchScalarGridSpec(
        num_scalar_prefetch=3,
        grid=(n_exp,),
        in_specs=[pl.BlockSpec(memory_space=pl.ANY),
                  pl.BlockSpec((1, d, hid), lambda e, *_: (e, 0, 0)),
                  pl.BlockSpec((1, d, hid), lambda e, *_: (e, 0, 0)),
                  pl.BlockSpec((1, hid, d), lambda e, *_: (e, 0, 0))],
        out_specs=pl.BlockSpec(memory_space=pl.ANY),
        scratch_shapes=[pltpu.VMEM((d, 2 * hid), BF16), pltpu.VMEM((hid, d), BF16),
                        pltpu.VMEM((EXPERT_RING * blk_rows, LANES), U32),
                        pltpu.VMEM((EXPERT_RING * blk_rows, LANES), U32),
                        pltpu.SemaphoreType.DMA((EXPERT_RING,)),
                        pltpu.SemaphoreType.DMA((EXPERT_RING,))],
    )
    return pl.pallas_call(
        _expert_kernel,
        grid_spec=grid_spec,
        out_shape=jax.ShapeDtypeStruct(xs.shape, U32),
        compiler_params=pltpu.CompilerParams(
            dimension_semantics=("arbitrary",), vmem_limit_bytes=VMEM_LIMIT_OTHER),
        name="experts",
    )(first_blk, n_blk, tail, xs, w_gate_e, w_up_e, w_down_e)


def _combine_kernel(tab_ref, x1_ref, bpos_ref, w_ref, ys_hbm, wsgu_ref, wsd_ref, g2_ref, b2_ref,
                    out_ref, buf, acc, bp_s, wq_s, sem, *, alpha):
    tt = x1_ref.shape[0]
    hid = wsd_ref.shape[0]
    sub = buf.shape[0] // BUF_ROWS
    chunk_rows = RUN_ROWS * sub
    n_exp = tab_ref.shape[2]
    cpb = SORT_BLOCK // RUN_ROWS

    def make_copy(chunk, slot):
        return pltpu.make_async_copy(
            ys_hbm.at[pl.ds(pl.multiple_of(slot * sub, 2 * sub), chunk_rows)],
            buf.at[pl.ds(pl.multiple_of(chunk * chunk_rows, chunk_rows), chunk_rows)],
            sem.at[chunk // cpb])

    @pl.when(pl.program_id(0) == 0)
    def _():
        buf[...] = jnp.zeros_like(buf)

    _start_chunks(tab_ref, n_exp, 1, make_copy)

    x1 = x1_ref[...]
    gu = jnp.dot(x1.astype(BF16), wsgu_ref[...], preferred_element_type=F32)
    hs = _silu(gu[:, :hid]) * gu[:, hid:]
    acc[...] = jnp.dot(hs.astype(BF16), wsd_ref[...], preferred_element_type=F32)

    for k in range(TOP_K):
        bp_s[k] = jnp.broadcast_to(bpos_ref[:, k:k + 1], (tt, LANES)).astype(I16)
        wq_s[k] = jnp.broadcast_to(w_ref[:, k:k + 1], (tt, LANES)).astype(BF16)

    def unsort_block(kb):
        qs = []
        for c0 in range(kb * SORT_BLOCK, (kb + 1) * SORT_BLOCK, LANES):
            iol = (lax.broadcasted_iota(I32, (tt, LANES), 1) + c0).astype(I16)
            q = jnp.zeros((tt, LANES), BF16)
            for k in range(TOP_K):
                q = jnp.where(iol == bp_s[k], wq_s[k], q)
            qs.append(q)
        rows = _load_packed_rows(buf, kb * SORT_BLOCK, SORT_BLOCK, sub)
        acc[...] += jnp.dot(jnp.concatenate(qs, axis=1), rows, preferred_element_type=F32)

    base_blocks = n_exp // cpb
    for kb in range(base_blocks):
        pltpu.make_async_copy(
            ys_hbm.at[pl.ds(0, cpb * chunk_rows)],
            buf.at[pl.ds(kb * cpb * chunk_rows, cpb * chunk_rows)], sem.at[kb]).wait()
        unsort_block(kb)

    n_ch = tab_ref[0, TAB_META, 0]
    for kb in range(base_blocks, BUF_ROWS // SORT_BLOCK):
        @pl.when(kb * cpb < n_ch)
        def _():
            def wait(j, c):
                make_copy(kb * cpb, 0).wait()
                return c
            lax.fori_loop(0, jnp.minimum(n_ch - kb * cpb, cpb), wait, 0)
            unsort_block(kb)

    out_ref[...] = _layernorm(alpha * x1 + acc[...], g2_ref[...], b2_ref[...])


def _combine_call(tab, x1, bpos_tok, w_tok, ys, ws_gu, ws_d, ln2_g, ln2_b, *, alpha):
    t, d = x1.shape
    tt = TILE
    n_tiles = t // tt
    n_exp = tab.shape[2]

    def const(shape):
        return pl.BlockSpec(shape, lambda i: (0,) * len(shape))

    tab_spec = lambda f: pl.BlockSpec((1, 4, n_exp), f, memory_space=pltpu.SMEM)
    return pl.pallas_call(
        functools.partial(_combine_kernel, alpha=alpha),
        grid=(n_tiles,),
        in_specs=[tab_spec(lambda i: (i, 0, 0)),
                  pl.BlockSpec((tt, d), lambda i: (i, 0)),
                  pl.BlockSpec((tt, TOP_K), lambda i: (i, 0)),
                  pl.BlockSpec((tt, TOP_K), lambda i: (i, 0)),
                  pl.BlockSpec(memory_space=pl.ANY),
                  const(ws_gu.shape), const(ws_d.shape), const(ln2_g.shape), const(ln2_b.shape)],
        out_specs=pl.BlockSpec((tt, d), lambda i: (i, 0)),
        out_shape=jax.ShapeDtypeStruct((t, d), F32),
        scratch_shapes=[pltpu.VMEM((BUF_ROWS * (d // 2 // LANES), LANES), U32),
                        pltpu.VMEM((tt, d), F32),
                        pltpu.VMEM((TOP_K, tt, LANES), I16),
                        pltpu.VMEM((TOP_K, tt, LANES), BF16),
                        pltpu.SemaphoreType.DMA((BUF_ROWS // SORT_BLOCK,))],
        compiler_params=pltpu.CompilerParams(
            dimension_semantics=("arbitrary",), vmem_limit_bytes=VMEM_LIMIT_OTHER),
        name="combine",
    )(tab, x1, bpos_tok, w_tok, ys, ws_gu, ws_d, ln2_g, ln2_b)


def _layer(x, w_in, pool_w, pool_scale, conv_dw, conv_ln_g, conv_ln_b, conv_w_out, w_out,
           ln1_g, ln1_b, w_router, router_bias, w_gate_e, w_up_e, w_down_e,
           ws_gate, ws_up, ws_down, ln2_g, ln2_b, *, alpha):
    b, s, d = x.shape
    t = b * s
    row = lambda v: v.reshape(1, -1)
    x1, logits_t = _mixer_call(
        x, w_in.astype(BF16), pool_w.astype(BF16), row(pool_scale), conv_dw, row(conv_ln_g),
        row(conv_ln_b), conv_w_out.astype(BF16), w_out.astype(BF16), row(ln1_g), row(ln1_b),
        w_router.T.astype(BF16), alpha=alpha)
    w_t, lpos_t, bpos_t, rtab, counts = _route_call(logits_t, router_bias.reshape(-1, 1))

    n = t * TOP_K
    n_blocks = -(-(n + (t // TILE) * N_EXPERTS + N_EXPERTS * (EXPERT_BLOCK - 1 + RUN_ROWS))
                 // EXPERT_BLOCK)
    n_slots = n_blocks * EXPERT_BLOCK
    counts = counts[0].astype(I32)
    n_blk = (counts + RUN_ROWS + EXPERT_BLOCK - 1) // EXPERT_BLOCK
    end_blk = jnp.cumsum(n_blk)
    first_blk = end_blk - n_blk
    pad_start = first_blk * EXPERT_BLOCK
    n_used = end_blk[-1]
    blk = jnp.arange(n_blocks, dtype=I32)
    block_e = jnp.minimum(jnp.sum((end_blk[None, :] <= blk[:, None]).astype(I32), axis=1),
                          N_EXPERTS - 1)
    real_end = (pad_start + counts)[block_e]
    zflag = ((blk >= n_used) | ((blk + 1) * EXPERT_BLOCK > real_end)).astype(I32)
    tail = jnp.stack([n_used, jnp.asarray(n_blocks, I32)])

    rtab = rtab.astype(I32)
    nch = rtab[:, ROUTE_NCH, :]
    meta = jnp.zeros_like(nch).at[:, 0].set(jnp.sum(nch, axis=1)).at[:, 1].set(jnp.max(nch, axis=1))
    slot = rtab[:, ROUTE_BASE, :] + pad_start[None, :]
    tab_d = jnp.stack([slot, nch, rtab[:, ROUTE_START, :], meta], axis=1)
    tab_c = jnp.stack([slot, nch, rtab[:, ROUTE_CHUNK, :], meta], axis=1)

    xs = _dispatch_call(zflag, tab_d, lpos_t, x1, n_slots)
    ys = _expert_call(first_blk, n_blk, tail, xs, w_gate_e, w_up_e, w_down_e)
    ws_gu = jnp.concatenate([ws_gate, ws_up], axis=1).astype(BF16)
    out = _combine_call(tab_c, x1, bpos_t.T, w_t.T, ys, ws_gu, ws_down.astype(BF16), row(ln2_g),
                        row(ln2_b), alpha=alpha)
    return out.reshape(b, s, d)


def kernel(x, w_in, pool_w, pool_scale, conv_dw, conv_ln_g, conv_ln_b, conv_w_out, w_out, ln1_g, ln1_b, w_router, router_bias, w_gate_e, w_up_e, w_down_e, ws_gate, ws_up, ws_down, ln2_g, ln2_b):
    depth = w_in.shape[0]
    alpha = (2.0 * depth) ** 0.25
    for l in range(depth):
        x = _layer(x, w_in[l], pool_w[l], pool_scale[l], conv_dw[l], conv_ln_g[l], conv_ln_b[l],
                   conv_w_out[l], w_out[l], ln1_g[l], ln1_b[l], w_router[l], router_bias[l],
                   w_gate_e[l], w_up_e[l], w_down_e[l], ws_gate[l], ws_up[l], ws_down[l],
                   ln2_g[l], ln2_b[l], alpha=alpha)
    return x
```

```python
import functools

import jax
import jax.numpy as jnp
from jax import lax
from jax.experimental import pallas as pl
from jax.experimental.pallas import tpu as pltpu

F32 = jnp.float32
BF16 = jnp.bfloat16
I32 = jnp.int32
U32 = jnp.uint32
I16 = jnp.int16

POOL_GROUPS = 4
POOL_WINDOWS = (2, 4, 8, 16)
CONV_KERNEL = 31
N_EXPERTS = 256
TOP_K = 8
N_GROUPS = 8
TOPK_GROUPS = 4
EXPERTS_PER_GROUP = N_EXPERTS // N_GROUPS
ROUTED_SCALE = 2.5
LN_EPS = 1e-5

SUBLANES = 8
LANES = 128
MIX_ROWS = 512
POOL_CHUNK = 32
CONV_CHUNK = 64
CONV_COLS = 256
POOL_HIST = 16
CONV_HIST = 32
TILE = 512
RUN_ROWS = 32
SORT_BLOCK = 512
MAX_CHUNKS = N_EXPERTS + TILE * TOP_K // RUN_ROWS
BUF_ROWS = MAX_CHUNKS * RUN_ROWS
SORT_ROWS = -(-(TILE * TOP_K + N_EXPERTS + RUN_ROWS) // SORT_BLOCK) * SORT_BLOCK
EXPERT_BLOCK = 256
EXPERT_RING = 6
VMEM_LIMIT_MIXER = 56 * 1024 * 1024
VMEM_LIMIT_OTHER = 56 * 1024 * 1024


def _layernorm(z, g, b):
    mu = jnp.mean(z, axis=-1, keepdims=True)
    d = z - mu
    var = jnp.mean(d * d, axis=-1, keepdims=True)
    return d * lax.rsqrt(var + LN_EPS) * g + b


def _silu(v):
    return v * jax.nn.sigmoid(v)


def _store_packed_rows(ref, row0, v):
    n, d = v.shape
    half = d // 2
    sub = half // LANES
    for j in range(sub):
        words = pltpu.pack_elementwise(
            [v[:, j * LANES:(j + 1) * LANES], v[:, half + j * LANES:half + (j + 1) * LANES]],
            packed_dtype=BF16)
        ref[pl.ds(row0 * sub + j, n, stride=sub), :] = words


def _load_packed_rows(ref, row0, n, sub):
    los, his = [], []
    for j in range(sub):
        words = ref[pl.ds(row0 * sub + j, n, stride=sub), :]
        los.append(pltpu.unpack_elementwise(words, index=0, packed_dtype=BF16,
                                            unpacked_dtype=F32).astype(BF16))
        his.append(pltpu.unpack_elementwise(words, index=1, packed_dtype=BF16,
                                            unpacked_dtype=F32).astype(BF16))
    return jnp.concatenate(los + his, axis=1)


def _mixer_kernel(x_ref, w_in_ref, pool_w_ref, pool_scale_ref, dw_ref, cg_ref, cb_ref, cwo_ref,
                  w_out_ref, g1_ref, b1_ref, wr_ref, x1_ref, lt_ref,
                  ubuf, vbuf, rbuf, cvbuf, *, alpha):
    ts = x_ref.shape[1]
    d_model = x_ref.shape[2]
    pw = ubuf.shape[1]
    cw = vbuf.shape[1]
    gi = pw // POOL_GROUPS
    si = pl.program_id(1)

    @pl.when(si == 0)
    def _():
        ubuf[0:POOL_HIST, :] = jnp.zeros((POOL_HIST, pw), F32)
        vbuf[0:CONV_HIST, :] = jnp.zeros((CONV_HIST, cw), F32)

    x = x_ref[0]
    xb = x.astype(BF16)
    ubuf[POOL_HIST:POOL_HIST + ts, :] = jnp.dot(xb, w_in_ref[:, 0:pw], preferred_element_type=F32)
    a = jnp.dot(xb, w_in_ref[:, pw:pw + 2 * cw], preferred_element_type=F32)
    vbuf[CONV_HIST:CONV_HIST + ts, :] = a[:, :cw] * jax.nn.sigmoid(a[:, cw:])
    pos0 = si * ts

    for c in range(ts // POOL_CHUNK):
        r0 = c * POOL_CHUNK
        pos = pos0 + r0 + lax.broadcasted_iota(I32, (POOL_CHUNK, 1), 0)
        for g, w in enumerate(POOL_WINDOWS):
            cols = slice(g * gi, (g + 1) * gi)
            cur = ubuf[pl.ds(POOL_HIST + r0, POOL_CHUNK), cols]
            s = cur
            for j in range(1, w):
                s = s + ubuf[pl.ds(POOL_HIST + r0 - j, POOL_CHUNK), cols]
            cnt = jnp.minimum(pos + 1, w).astype(F32)
            rbuf[pl.ds(r0, POOL_CHUNK), cols] = (s / cnt - cur).astype(BF16)

    for c in range(ts // CONV_CHUNK):
        r0 = c * CONV_CHUNK
        base = CONV_HIST + r0 - (CONV_KERNEL - 1)
        for c0 in range(0, cw, CONV_COLS):
            cols = slice(c0, c0 + CONV_COLS)
            parts = []
            for rho in range(SUBLANES):
                taps = range(rho, CONV_KERNEL, SUBLANES)
                win = vbuf[pl.ds(base + rho, CONV_CHUNK + taps[-1] - rho), cols]
                part = None
                for k in taps:
                    term = dw_ref[k:k + 1, cols] * win[k - rho:k - rho + CONV_CHUNK, :]
                    part = term if part is None else part + term
                parts.append(part)
            cvbuf[pl.ds(r0, CONV_CHUNK), cols] = functools.reduce(lambda p, q: p + q, parts)

    ubuf[0:POOL_HIST, :] = ubuf[ts:ts + POOL_HIST, :]
    vbuf[0:CONV_HIST, :] = vbuf[ts:ts + CONV_HIST, :]

    y_pool = jnp.concatenate(
        [jnp.dot(rbuf[:, g * gi:(g + 1) * gi], pool_w_ref[g], preferred_element_type=F32)
         for g in range(POOL_GROUPS)], axis=1) * pool_scale_ref[...]
    conv = _silu(_layernorm(cvbuf[...], cg_ref[...], cb_ref[...]))
    y_conv = jnp.dot(conv.astype(BF16), cwo_ref[...], preferred_element_type=F32)
    gates = jnp.dot(xb, w_in_ref[:, pw + 2 * cw:], preferred_element_type=F32)
    merged = (jax.nn.sigmoid(gates[:, :d_model]) * y_pool
              + jax.nn.sigmoid(gates[:, d_model:]) * y_conv)
    m = jnp.dot(merged.astype(BF16), w_out_ref[...], preferred_element_type=F32)
    x1 = _layernorm(alpha * x + m, g1_ref[...], b1_ref[...])
    x1_ref[...] = x1
    lt_ref[...] = lax.dot_general(wr_ref[...], x1.astype(BF16), (((1,), (1,)), ((), ())),
                                  preferred_element_type=F32)


def _mixer_call(x, w_in, pool_w, pool_scale, conv_dw, cln_g, cln_b, conv_w_out, w_out,
                ln1_g, ln1_b, w_router_t, *, alpha):
    b, s, d = x.shape
    ts = MIX_ROWS
    ns = s // ts
    pw = pool_w.shape[0] * pool_w.shape[1]
    cw = conv_dw.shape[1]
    n_exp = w_router_t.shape[0]

    def const(shape):
        return pl.BlockSpec(shape, lambda bi, si: (0,) * len(shape))

    return pl.pallas_call(
        functools.partial(_mixer_kernel, alpha=alpha),
        grid=(b, ns),
        in_specs=[
            pl.BlockSpec((1, ts, d), lambda bi, si: (bi, si, 0)),
            const(w_in.shape), const(pool_w.shape), const(pool_scale.shape), const(conv_dw.shape),
            const(cln_g.shape), const(cln_b.shape), const(conv_w_out.shape), const(w_out.shape),
            const(ln1_g.shape), const(ln1_b.shape), const(w_router_t.shape),
        ],
        out_specs=[
            pl.BlockSpec((ts, d), lambda bi, si: (bi * ns + si, 0)),
            pl.BlockSpec((n_exp, ts), lambda bi, si: (0, bi * ns + si)),
        ],
        out_shape=[jax.ShapeDtypeStruct((b * s, d), F32),
                   jax.ShapeDtypeStruct((n_exp, b * s), F32)],
        scratch_shapes=[
            pltpu.VMEM((POOL_HIST + ts, pw), F32),
            pltpu.VMEM((CONV_HIST + ts, cw), F32),
            pltpu.VMEM((ts, pw), BF16),
            pltpu.VMEM((ts, cw), F32),
        ],
        compiler_params=pltpu.CompilerParams(
            dimension_semantics=("arbitrary", "arbitrary"), vmem_limit_bytes=VMEM_LIMIT_MIXER),
        name="mixer",
    )(x, w_in, pool_w, pool_scale, conv_dw, cln_g, cln_b, conv_w_out, w_out, ln1_g, ln1_b,
      w_router_t)


ROUTE_BASE = 0
ROUTE_NCH = 1
ROUTE_CHUNK = 2
ROUTE_START = 3


def _route_kernel(lt_ref, bias_ref, w_ref, lpos_ref, bpos_ref, tab_ref, cnt_ref, carry):
    n_exp, tt = lt_ref.shape
    neg = -jnp.inf

    @pl.when(pl.program_id(0) == 0)
    def _():
        carry[...] = jnp.zeros_like(carry)

    scores = jax.nn.sigmoid(lt_ref[...])
    sel = scores + bias_ref[...]
    sel3 = sel.reshape(N_GROUPS, EXPERTS_PER_GROUP, tt)
    io3 = lax.broadcasted_iota(I32, sel3.shape, 1)
    m1 = jnp.max(sel3, axis=1, keepdims=True)
    i1 = jnp.min(jnp.where(sel3 == m1, io3, EXPERTS_PER_GROUP), axis=1, keepdims=True)
    m2 = jnp.max(jnp.where(io3 == i1, neg, sel3), axis=1, keepdims=True)
    gscore = m1 + m2
    iog = lax.broadcasted_iota(I32, gscore.shape, 0)
    gsel = jnp.zeros(gscore.shape, F32)
    for _ in range(TOPK_GROUPS):
        m = jnp.max(gscore, axis=0, keepdims=True)
        gi = jnp.min(jnp.where(gscore == m, iog, N_GROUPS), axis=0, keepdims=True)
        hit = iog == gi
        gsel = jnp.where(hit, 1.0, gsel)
        gscore = jnp.where(hit, neg, gscore)
    val = jnp.where(gsel > 0.5, sel3, neg).reshape(n_exp, tt)
    ioe = lax.broadcasted_iota(I32, (n_exp, tt), 0)
    member = jnp.zeros((n_exp, tt), F32)
    idxs, ws = [], []
    for _ in range(TOP_K):
        m = jnp.max(val, axis=0, keepdims=True)
        ei = jnp.min(jnp.where(val == m, ioe, n_exp), axis=0, keepdims=True)
        hit = ioe == ei
        idxs.append(ei)
        ws.append(jnp.sum(jnp.where(hit, scores, 0.0), axis=0, keepdims=True))
        member = jnp.where(hit, 1.0, member)
        val = jnp.where(hit, neg, val)
    w = jnp.concatenate(ws, axis=0)
    w_ref[...] = w / jnp.sum(w, axis=0, keepdims=True) * ROUTED_SCALE

    mb = member.astype(BF16)
    before = (lax.broadcasted_iota(I32, (tt, tt), 0) < lax.broadcasted_iota(I32, (tt, tt), 1))
    rank_in_tile = jnp.dot(mb, before.astype(BF16), preferred_element_type=F32)
    c_col = jnp.sum(member, axis=1, keepdims=True)
    c_row = lax.dot_general(jnp.ones((SUBLANES, tt), BF16), mb, (((1,), (1,)), ((), ())),
                            preferred_element_type=F32)

    def even(c):
        return c + (c - 2.0 * jnp.floor(c * 0.5))

    def chunks(c):
        return jnp.maximum(jnp.floor((c + (RUN_ROWS - 1)) * (1.0 / RUN_ROWS)), 1.0)

    ee0 = lax.broadcasted_iota(I32, (n_exp, n_exp), 0)
    ee1 = lax.broadcasted_iota(I32, (n_exp, n_exp), 1)
    lower = (ee1 < ee0).astype(BF16)
    upper = (ee0 < ee1).astype(BF16)

    def prefix_col(v):
        return jnp.dot(lower, jnp.broadcast_to(v, (n_exp, LANES)).astype(BF16),
                       preferred_element_type=F32)[:, 0:1]

    def prefix_row(v):
        return jnp.dot(v.astype(BF16), upper, preferred_element_type=F32)

    chunk_col = prefix_col(chunks(c_col))
    start_col = prefix_col(even(c_col))
    bfull = chunk_col * RUN_ROWS + rank_in_tile
    lfull = start_col + rank_in_tile
    bpos_ref[...] = jnp.concatenate(
        [jnp.sum(jnp.where(ioe == ei, bfull, 0.0), axis=0, keepdims=True) for ei in idxs],
        axis=0).astype(I32)
    lpos_ref[...] = jnp.concatenate(
        [jnp.sum(jnp.where(ioe == ei, lfull, 0.0), axis=0, keepdims=True) for ei in idxs],
        axis=0).astype(I32)
    nch_row = chunks(c_row)
    tab_ref[0] = jnp.concatenate(
        [carry[0:1], nch_row[0:1], prefix_row(nch_row)[0:1], prefix_row(even(c_row))[0:1],
         jnp.zeros((SUBLANES - 4, n_exp), F32)], axis=0)
    carry[...] = carry[...] + even(c_row)
    cnt_ref[...] = carry[...]


def _route_call(logits_t, bias_col):
    n_exp, t = logits_t.shape
    tt = TILE
    return pl.pallas_call(
        _route_kernel,
        grid=(t // tt,),
        in_specs=[pl.BlockSpec((n_exp, tt), lambda i: (0, i)),
                  pl.BlockSpec((n_exp, 1), lambda i: (0, 0))],
        out_specs=[pl.BlockSpec((TOP_K, tt), lambda i: (0, i)),
                   pl.BlockSpec((TOP_K, tt), lambda i: (0, i)),
                   pl.BlockSpec((TOP_K, tt), lambda i: (0, i)),
                   pl.BlockSpec((1, SUBLANES, n_exp), lambda i: (i, 0, 0)),
                   pl.BlockSpec((SUBLANES, n_exp), lambda i: (0, 0))],
        out_shape=[jax.ShapeDtypeStruct((TOP_K, t), F32),
                   jax.ShapeDtypeStruct((TOP_K, t), I32),
                   jax.ShapeDtypeStruct((TOP_K, t), I32),
                   jax.ShapeDtypeStruct((t // tt, SUBLANES, n_exp), F32),
                   jax.ShapeDtypeStruct((SUBLANES, n_exp), F32)],
        scratch_shapes=[pltpu.VMEM((SUBLANES, n_exp), F32)],
        compiler_params=pltpu.CompilerParams(
            dimension_semantics=("arbitrary",), vmem_limit_bytes=VMEM_LIMIT_OTHER),
        name="route",
    )(logits_t, bias_col)


TAB_SLOT = 0
TAB_NCH = 1
TAB_SRC = 2
TAB_META = 3


def _start_chunks(tab_ref, n_exp, src_step, make_copy):
    def first(h, c):
        for u in range(2):
            e = 2 * h + u
            make_copy(tab_ref[0, TAB_SRC, e], tab_ref[0, TAB_SLOT, e]).start(priority=u)
        return c

    lax.fori_loop(0, n_exp // 2, first, 0, unroll=4)

    @pl.when(tab_ref[0, TAB_META, 1] > 1)
    def _():
        def extra(e, c):
            def one(i, c2):
                make_copy(tab_ref[0, TAB_SRC, e] + i * src_step,
                          tab_ref[0, TAB_SLOT, e] + i * RUN_ROWS).start()
                return c2
            lax.fori_loop(1, tab_ref[0, TAB_NCH, e], one, 0)
            return c
        lax.fori_loop(0, n_exp, extra, 0)


def _wait_chunks(tab_ref, make_copy):
    def wait(i, c):
        make_copy(0, 0).wait()
        return c
    lax.fori_loop(0, tab_ref[0, TAB_META, 0], wait, 0)


def _dispatch_kernel(zflag_ref, tab_ref, prev_tab_ref, lpos_ref, x1_ref, xs_hbm,
                     sbuf, zbuf, zsem, sem):
    tt = x1_ref.shape[0]
    n_blocks = zflag_ref.shape[0]
    n_exp = tab_ref.shape[2]
    sub = zbuf.shape[0] // EXPERT_BLOCK
    chunk_rows = RUN_ROWS * sub
    i = pl.program_id(0)

    def zero_copy(blk):
        return pltpu.make_async_copy(
            zbuf, xs_hbm.at[pl.ds(pl.multiple_of(blk * zbuf.shape[0], zbuf.shape[0]),
                                  zbuf.shape[0])], zsem)

    @pl.when(i == 0)
    def _():
        zbuf[...] = jnp.zeros_like(zbuf)

        def start(blk, c):
            @pl.when(zflag_ref[blk] == 1)
            def _():
                zero_copy(blk).start()
            return c

        def wait(blk, c):
            @pl.when(zflag_ref[blk] == 1)
            def _():
                zero_copy(blk).wait()
            return c

        lax.fori_loop(0, n_blocks, start, 0)
        lax.fori_loop(0, n_blocks, wait, 0)

    def make_copy_from(buf):
        def make_copy(row, slot):
            return pltpu.make_async_copy(
                buf.at[pl.ds(pl.multiple_of(row * sub, 2 * sub), chunk_rows)],
                xs_hbm.at[pl.ds(pl.multiple_of(slot * sub, 2 * sub), chunk_rows)], sem)
        return make_copy

    xb = x1_ref[...].astype(BF16)

    def sort_into(buf):
        for rb in range(SORT_ROWS // SORT_BLOCK):
            ior = lax.broadcasted_iota(I32, (SORT_BLOCK, tt), 0) + rb * SORT_BLOCK
            p = jnp.zeros((SORT_BLOCK, tt), F32)
            for k in range(TOP_K):
                p = jnp.where(ior == lpos_ref[k:k + 1, :], 1.0, p)
            _store_packed_rows(buf, rb * SORT_BLOCK,
                               jnp.dot(p.astype(BF16), xb, preferred_element_type=F32))

    for s in range(2):
        @pl.when(i % 2 == s)
        def _():
            sort_into(sbuf.at[s])

    @pl.when(i > 0)
    def _():
        _wait_chunks(prev_tab_ref, make_copy_from(sbuf.at[0]))

    for s in range(2):
        @pl.when(i % 2 == s)
        def _():
            _start_chunks(tab_ref, n_exp, RUN_ROWS, make_copy_from(sbuf.at[s]))

    @pl.when(i == pl.num_programs(0) - 1)
    def _():
        _wait_chunks(tab_ref, make_copy_from(sbuf.at[0]))


def _dispatch_call(zflag, tab, lpos, x1, n_slots):
    t, d = x1.shape
    tt = TILE
    n_exp = tab.shape[2]
    sub = d // 2 // LANES
    tab_spec = lambda f: pl.BlockSpec((1, 4, n_exp), f, memory_space=pltpu.SMEM)
    grid_spec = pltpu.PrefetchScalarGridSpec(
        num_scalar_prefetch=1,
        grid=(t // tt,),
        in_specs=[tab_spec(lambda i, zf: (i, 0, 0)),
                  tab_spec(lambda i, zf: (jnp.maximum(i - 1, 0), 0, 0)),
                  pl.BlockSpec((TOP_K, tt), lambda i, zf: (0, i)),
                  pl.BlockSpec((tt, d), lambda i, zf: (i, 0))],
        out_specs=pl.BlockSpec(memory_space=pl.ANY),
        scratch_shapes=[pltpu.VMEM((2, SORT_ROWS * sub, LANES), U32),
                        pltpu.VMEM((EXPERT_BLOCK * sub, LANES), U32),
                        pltpu.SemaphoreType.DMA(()),
                        pltpu.SemaphoreType.DMA(())],
    )
    return pl.pallas_call(
        _dispatch_kernel,
        grid_spec=grid_spec,
        out_shape=jax.ShapeDtypeStruct((n_slots * sub, LANES), U32),
        compiler_params=pltpu.CompilerParams(
            dimension_semantics=("arbitrary",), vmem_limit_bytes=VMEM_LIMIT_OTHER),
        name="dispatch",
    )(zflag, tab, tab, lpos, x1)


def _expert_kernel(first_ref, nblk_ref, tail_ref, xs_hbm, wg_ref, wu_ref, wd_ref, ys_hbm,
                   wgu_s, wd_s, xbuf, ybuf, isem, osem):
    e = pl.program_id(0)
    last = pl.num_programs(0) - 1
    hid = wg_ref.shape[2]
    blk_rows = xbuf.shape[0] // EXPERT_RING
    sub = blk_rows // EXPERT_BLOCK
    n_used = tail_ref[0]

    def rows_of(blk):
        return pl.ds(pl.multiple_of(blk * blk_rows, blk_rows), blk_rows)

    def in_copy(g):
        s = g % EXPERT_RING
        return pltpu.make_async_copy(xs_hbm.at[rows_of(g)], xbuf.at[rows_of(s)], isem.at[s])

    def out_copy(g):
        s = g % EXPERT_RING
        return pltpu.make_async_copy(ybuf.at[rows_of(s)], ys_hbm.at[rows_of(g)], osem.at[s])

    @pl.when(e == 0)
    def _():
        for g in range(EXPERT_RING):
            in_copy(g).start()

    wgu_s[:, 0:hid] = wg_ref[0].astype(BF16)
    wgu_s[:, hid:2 * hid] = wu_ref[0].astype(BF16)
    wd_s[...] = wd_ref[0].astype(BF16)

    def consume(g0, m):
        gs = [g0 + j for j in range(m)]
        for g in gs:
            in_copy(g).wait()

            @pl.when(g >= EXPERT_RING)
            def _():
                out_copy(g).wait()
        row0 = [(g % EXPERT_RING) * EXPERT_BLOCK for g in gs]
        xb = jnp.concatenate([_load_packed_rows(xbuf, r, EXPERT_BLOCK, sub) for r in row0], axis=0)
        gu = jnp.dot(xb, wgu_s[...], preferred_element_type=F32)
        h = _silu(gu[:, :hid]) * gu[:, hid:]
        y = jnp.dot(h.astype(BF16), wd_s[...], preferred_element_type=F32)
        for j, g in enumerate(gs):
            _store_packed_rows(ybuf, row0[j], y[j * EXPERT_BLOCK:(j + 1) * EXPERT_BLOCK])
            out_copy(g).start()

            @pl.when(g + EXPERT_RING < n_used)
            def _():
                in_copy(g + EXPERT_RING).start()

    first = first_ref[e]
    n = nblk_ref[e]

    def pair(p, c):
        consume(first + 2 * p, 2)
        return c

    lax.fori_loop(0, n // 2, pair, 0)

    @pl.when(n % 2 == 1)
    def _():
        consume(first + n - 1, 1)

    @pl.when(e == last)
    def _():
        for g in range(EXPERT_RING):
            out_copy(g).wait()
        ybuf[0:blk_rows, :] = jnp.zeros((blk_rows, ybuf.shape[1]), ybuf.dtype)

        def start(b, c):
            pltpu.make_async_copy(ybuf.at[rows_of(0)], ys_hbm.at[rows_of(b)], osem.at[0]).start()
            return c

        def wait(b, c):
            pltpu.make_async_copy(ybuf.at[rows_of(0)], ys_hbm.at[rows_of(b)], osem.at[0]).wait()
            return c

        lax.fori_loop(tail_ref[0], tail_ref[1], start, 0)
        lax.fori_loop(tail_ref[0], tail_ref[1], wait, 0)


def _expert_call(first_blk, n_blk, tail, xs, w_gate_e, w_up_e, w_down_e):
    n_exp, d, hid = w_gate_e.shape
    blk_rows = EXPERT_BLOCK * (d // 2 // LANES)
    grid_spec = pltpu.PrefetchScalarGridSpec(
        num_scalar_prefetch=3,
        grid=(n_exp,),
        in_specs=[pl.BlockSpec(memory_space=pl.ANY),
                  pl.BlockSpec((1, d, hid), lambda e, *_: (e, 0, 0)),
                  pl.BlockSpec((1, d, hid), lambda e, *_: (e, 0, 0)),
                  pl.BlockSpec((1, hid, d), lambda e, *_: (e, 0, 0))],
        out_specs=pl.BlockSpec(memory_space=pl.ANY),
        scratch_shapes=[pltpu.VMEM((d, 2 * hid), BF16), pltpu.VMEM((hid, d), BF16),
                        pltpu.VMEM((EXPERT_RING * blk_rows, LANES), U32),
                        pltpu.VMEM((EXPERT_RING * blk_rows, LANES), U32),
                        pltpu.SemaphoreType.DMA((EXPERT_RING,)),
                        pltpu.SemaphoreType.DMA((EXPERT_RING,))],
    )
    return pl.pallas_call(
        _expert_kernel,
        grid_spec=grid_spec,
        out_shape=jax.ShapeDtypeStruct(xs.shape, U32),
        compiler_params=pltpu.CompilerParams(
            dimension_semantics=("arbitrary",), vmem_limit_bytes=VMEM_LIMIT_OTHER),
        name="experts",
    )(first_blk, n_blk, tail, xs, w_gate_e, w_up_e, w_down_e)


def _combine_kernel(tab_ref, x1_ref, bpos_ref, w_ref, ys_hbm, wsgu_ref, wsd_ref, g2_ref, b2_ref,
                    out_ref, buf, acc, bp_s, wq_s, sem, *, alpha):
    tt = x1_ref.shape[0]
    hid = wsd_ref.shape[0]
    sub = buf.shape[0] // BUF_ROWS
    chunk_rows = RUN_ROWS * sub
    n_exp = tab_ref.shape[2]
    cpb = SORT_BLOCK // RUN_ROWS

    def make_copy(chunk, slot):
        return pltpu.make_async_copy(
            ys_hbm.at[pl.ds(pl.multiple_of(slot * sub, 2 * sub), chunk_rows)],
            buf.at[pl.ds(pl.multiple_of(chunk * chunk_rows, chunk_rows), chunk_rows)],
            sem.at[chunk // cpb])

    @pl.when(pl.program_id(0) == 0)
    def _():
        buf[...] = jnp.zeros_like(buf)

    _start_chunks(tab_ref, n_exp, 1, make_copy)

    x1 = x1_ref[...]
    gu = jnp.dot(x1.astype(BF16), wsgu_ref[...], preferred_element_type=F32)
    hs = _silu(gu[:, :hid]) * gu[:, hid:]
    acc[...] = jnp.dot(hs.astype(BF16), wsd_ref[...], preferred_element_type=F32)

    for k in range(TOP_K):
        bp_s[k] = jnp.broadcast_to(bpos_ref[:, k:k + 1], (tt, LANES)).astype(I16)
        wq_s[k] = jnp.broadcast_to(w_ref[:, k:k + 1], (tt, LANES)).astype(BF16)

    def unsort_block(kb):
        qs = []
        for c0 in range(kb * SORT_BLOCK, (kb + 1) * SORT_BLOCK, LANES):
            iol = (lax.broadcasted_iota(I32, (tt, LANES), 1) + c0).astype(I16)
            q = jnp.zeros((tt, LANES), BF16)
            for k in range(TOP_K):
                q = jnp.where(iol == bp_s[k], wq_s[k], q)
            qs.append(q)
        rows = _load_packed_rows(buf, kb * SORT_BLOCK, SORT_BLOCK, sub)
        acc[...] += jnp.dot(jnp.concatenate(qs, axis=1), rows, preferred_element_type=F32)

    base_blocks = n_exp // cpb
    for kb in range(base_blocks):
        pltpu.make_async_copy(
            ys_hbm.at[pl.ds(0, cpb * chunk_rows)],
            buf.at[pl.ds(kb * cpb * chunk_rows, cpb * chunk_rows)], sem.at[kb]).wait()
        unsort_block(kb)

    n_ch = tab_ref[0, TAB_META, 0]
    for kb in range(base_blocks, BUF_ROWS // SORT_BLOCK):
        @pl.when(kb * cpb < n_ch)
        def _():
            def wait(j, c):
                make_copy(kb * cpb, 0).wait()
                return c
            lax.fori_loop(0, jnp.minimum(n_ch - kb * cpb, cpb), wait, 0)
            unsort_block(kb)

    out_ref[...] = _layernorm(alpha * x1 + acc[...], g2_ref[...], b2_ref[...])


def _combine_call(tab, x1, bpos_tok, w_tok, ys, ws_gu, ws_d, ln2_g, ln2_b, *, alpha):
    t, d = x1.shape
    tt = TILE
    n_tiles = t // tt
    n_exp = tab.shape[2]

    def const(shape):
        return pl.BlockSpec(shape, lambda i: (0,) * len(shape))

    tab_spec = lambda f: pl.BlockSpec((1, 4, n_exp), f, memory_space=pltpu.SMEM)
    return pl.pallas_call(
        functools.partial(_combine_kernel, alpha=alpha),
        grid=(n_tiles,),
        in_specs=[tab_spec(lambda i: (i, 0, 0)),
                  pl.BlockSpec((tt, d), lambda i: (i, 0)),
                  pl.BlockSpec((tt, TOP_K), lambda i: (i, 0)),
                  pl.BlockSpec((tt, TOP_K), lambda i: (i, 0)),
                  pl.BlockSpec(memory_space=pl.ANY),
                  const(ws_gu.shape), const(ws_d.shape), const(ln2_g.shape), const(ln2_b.shape)],
        out_specs=pl.BlockSpec((tt, d), lambda i: (i, 0)),
        out_shape=jax.ShapeDtypeStruct((t, d), F32),
        scratch_shapes=[pltpu.VMEM((BUF_ROWS * (d // 2 // LANES), LANES), U32),
                        pltpu.VMEM((tt, d), F32),
                        pltpu.VMEM((TOP_K, tt, LANES), I16),
                        pltpu.VMEM((TOP_K, tt, LANES), BF16),
                        pltpu.SemaphoreType.DMA((BUF_ROWS // SORT_BLOCK,))],
        compiler_params=pltpu.CompilerParams(
            dimension_semantics=("arbitrary",), vmem_limit_bytes=VMEM_LIMIT_OTHER),
        name="combine",
    )(tab, x1, bpos_tok, w_tok, ys, ws_gu, ws_d, ln2_g, ln2_b)


def _layer(x, w_in, pool_w, pool_scale, conv_dw, conv_ln_g, conv_ln_b, conv_w_out, w_out,
           ln1_g, ln1_b, w_router, router_bias, w_gate_e, w_up_e, w_down_e,
           ws_gate, ws_up, ws_down, ln2_g, ln2_b, *, alpha):
    b, s, d = x.shape
    t = b * s
    row = lambda v: v.reshape(1, -1)
    x1, logits_t = _mixer_call(
        x, w_in.astype(BF16), pool_w.astype(BF16), row(pool_scale), conv_dw, row(conv_ln_g),
        row(conv_ln_b), conv_w_out.astype(BF16), w_out.astype(BF16), row(ln1_g), row(ln1_b),
        w_router.T.astype(BF16), alpha=alpha)
    w_t, lpos_t, bpos_t, rtab, counts = _route_call(logits_t, router_bias.reshape(-1, 1))

    n = t * TOP_K
    n_blocks = -(-(n + (t // TILE) * N_EXPERTS + N_EXPERTS * (EXPERT_BLOCK - 1 + RUN_ROWS))
                 // EXPERT_BLOCK)
    n_slots = n_blocks * EXPERT_BLOCK
    counts = counts[0].astype(I32)
    n_blk = (counts + RUN_ROWS + EXPERT_BLOCK - 1) // EXPERT_BLOCK
    end_blk = jnp.cumsum(n_blk)
    first_blk = end_blk - n_blk
    pad_start = first_blk * EXPERT_BLOCK
    n_used = end_blk[-1]
    blk = jnp.arange(n_blocks, dtype=I32)
    block_e = jnp.minimum(jnp.sum((end_blk[None, :] <= blk[:, None]).astype(I32), axis=1),
                          N_EXPERTS - 1)
    real_end = (pad_start + counts)[block_e]
    zflag = ((blk >= n_used) | ((blk + 1) * EXPERT_BLOCK > real_end)).astype(I32)
    tail = jnp.stack([n_used, jnp.asarray(n_blocks, I32)])

    rtab = rtab.astype(I32)
    nch = rtab[:, ROUTE_NCH, :]
    meta = jnp.zeros_like(nch).at[:, 0].set(jnp.sum(nch, axis=1)).at[:, 1].set(jnp.max(nch, axis=1))
    slot = rtab[:, ROUTE_BASE, :] + pad_start[None, :]
    tab_d = jnp.stack([slot, nch, rtab[:, ROUTE_START, :], meta], axis=1)
    tab_c = jnp.stack([slot, nch, rtab[:, ROUTE_CHUNK, :], meta], axis=1)

    xs = _dispatch_call(zflag, tab_d, lpos_t, x1, n_slots)
    ys = _expert_call(first_blk, n_blk, tail, xs, w_gate_e, w_up_e, w_down_e)
    ws_gu = jnp.concatenate([ws_gate, ws_up], axis=1).astype(BF16)
    out = _combine_call(tab_c, x1, bpos_t.T, w_t.T, ys, ws_gu, ws_down.astype(BF16), row(ln2_g),
                        row(ln2_b), alpha=alpha)
    return out.reshape(b, s, d)


def kernel(x, w_in, pool_w, pool_scale, conv_dw, conv_ln_g, conv_ln_b, conv_w_out, w_out, ln1_g, ln1_b, w_router, router_bias, w_gate_e, w_up_e, w_down_e, ws_gate, ws_up, ws_down, ln2_g, ln2_b):
    depth = w_in.shape[0]
    alpha = (2.0 * depth) ** 0.25
    for l in range(depth):
        x = _layer(x, w_in[l], pool_w[l], pool_scale[l], conv_dw[l], conv_ln_g[l], conv_ln_b[l],
                   conv_w_out[l], w_out[l], ln1_g[l], ln1_b[l], w_router[l], router_bias[l],
                   w_gate_e[l], w_up_e[l], w_down_e[l], ws_gate[l], ws_up[l], ws_down[l],
                   ln2_g[l], ln2_b[l], alpha=alpha)
    return x
```

```python
import functools

import jax
import jax.numpy as jnp
from jax import lax
from jax.experimental import pallas as pl
from jax.experimental.pallas import tpu as pltpu

F32 = jnp.float32
BF16 = jnp.bfloat16
I32 = jnp.int32
U32 = jnp.uint32
I16 = jnp.int16

POOL_GROUPS = 4
POOL_WINDOWS = (2, 4, 8, 16)
CONV_KERNEL = 31
N_EXPERTS = 256
TOP_K = 8
N_GROUPS = 8
TOPK_GROUPS = 4
EXPERTS_PER_GROUP = N_EXPERTS // N_GROUPS
ROUTED_SCALE = 2.5
LN_EPS = 1e-5

SUBLANES = 8
LANES = 128
MIX_ROWS = 512
POOL_CHUNK = 32
CONV_CHUNK = 64
CONV_COLS = 256
POOL_HIST = 16
CONV_HIST = 32
TILE = 512
RUN_ROWS = 32
SORT_BLOCK = 512
MAX_CHUNKS = N_EXPERTS + TILE * TOP_K // RUN_ROWS
BUF_ROWS = MAX_CHUNKS * RUN_ROWS
SORT_ROWS = -(-(TILE * TOP_K + N_EXPERTS + RUN_ROWS) // SORT_BLOCK) * SORT_BLOCK
EXPERT_BLOCK = 256
EXPERT_RING = 6
VMEM_LIMIT_MIXER = 56 * 1024 * 1024
VMEM_LIMIT_OTHER = 56 * 1024 * 1024


def _layernorm(z, g, b):
    mu = jnp.mean(z, axis=-1, keepdims=True)
    d = z - mu
    var = jnp.mean(d * d, axis=-1, keepdims=True)
    return d * lax.rsqrt(var + LN_EPS) * g + b


def _silu(v):
    return v * jax.nn.sigmoid(v)


def _store_packed_rows(ref, row0, v):
    n, d = v.shape
    half = d // 2
    sub = half // LANES
    for j in range(sub):
        words = pltpu.pack_elementwise(
            [v[:, j * LANES:(j + 1) * LANES], v[:, half + j * LANES:half + (j + 1) * LANES]],
            packed_dtype=BF16)
        ref[pl.ds(row0 * sub + j, n, stride=sub), :] = words


def _load_packed_rows(ref, row0, n, sub):
    los, his = [], []
    for j in range(sub):
        words = ref[pl.ds(row0 * sub + j, n, stride=sub), :]
        los.append(pltpu.unpack_elementwise(words, index=0, packed_dtype=BF16,
                                            unpacked_dtype=F32).astype(BF16))
        his.append(pltpu.unpack_elementwise(words, index=1, packed_dtype=BF16,
                                            unpacked_dtype=F32).astype(BF16))
    return jnp.concatenate(los + his, axis=1)


def _mixer_kernel(x_ref, w_in_ref, pool_w_ref, pool_scale_ref, dw_ref, cg_ref, cb_ref, cwo_ref,
                  w_out_ref, g1_ref, b1_ref, wr_ref, x1_ref, lt_ref,
                  ubuf, vbuf, rbuf, cvbuf, *, alpha):
    ts = x_ref.shape[1]
    d_model = x_ref.shape[2]
    pw = ubuf.shape[1]
    cw = vbuf.shape[1]
    gi = pw // POOL_GROUPS
    si = pl.program_id(1)

    @pl.when(si == 0)
    def _():
        ubuf[0:POOL_HIST, :] = jnp.zeros((POOL_HIST, pw), F32)
        vbuf[0:CONV_HIST, :] = jnp.zeros((CONV_HIST, cw), F32)

    x = x_ref[0]
    xb = x.astype(BF16)
    ubuf[POOL_HIST:POOL_HIST + ts, :] = jnp.dot(xb, w_in_ref[:, 0:pw], preferred_element_type=F32)
    a = jnp.dot(xb, w_in_ref[:, pw:pw + 2 * cw], preferred_element_type=F32)
    vbuf[CONV_HIST:CONV_HIST + ts, :] = a[:, :cw] * jax.nn.sigmoid(a[:, cw:])
    pos0 = si * ts

    for c in range(ts // POOL_CHUNK):
        r0 = c * POOL_CHUNK
        pos = pos0 + r0 + lax.broadcasted_iota(I32, (POOL_CHUNK, 1), 0)
        for g, w in enumerate(POOL_WINDOWS):
            cols = slice(g * gi, (g + 1) * gi)
            cur = ubuf[pl.ds(POOL_HIST + r0, POOL_CHUNK), cols]
            s = cur
            for j in range(1, w):
                s = s + ubuf[pl.ds(POOL_HIST + r0 - j, POOL_CHUNK), cols]
            cnt = jnp.minimum(pos + 1, w).astype(F32)
            rbuf[pl.ds(r0, POOL_CHUNK), cols] = (s / cnt - cur).astype(BF16)

    for c in range(ts // CONV_CHUNK):
        r0 = c * CONV_CHUNK
        base = CONV_HIST + r0 - (CONV_KERNEL - 1)
        for c0 in range(0, cw, CONV_COLS):
            cols = slice(c0, c0 + CONV_COLS)
            parts = []
            for rho in range(SUBLANES):
                taps = range(rho, CONV_KERNEL, SUBLANES)
                win = vbuf[pl.ds(base + rho, CONV_CHUNK + taps[-1] - rho), cols]
                part = None
                for k in taps:
                    term = dw_ref[k:k + 1, cols] * win[k - rho:k - rho + CONV_CHUNK, :]
                    part = term if part is None else part + term
                parts.append(part)
            cvbuf[pl.ds(r0, CONV_CHUNK), cols] = functools.reduce(lambda p, q: p + q, parts)

    ubuf[0:POOL_HIST, :] = ubuf[ts:ts + POOL_HIST, :]
    vbuf[0:CONV_HIST, :] = vbuf[ts:ts + CONV_HIST, :]

    y_pool = jnp.concatenate(
        [jnp.dot(rbuf[:, g * gi:(g + 1) * gi], pool_w_ref[g], preferred_element_type=F32)
         for g in range(POOL_GROUPS)], axis=1) * pool_scale_ref[...]
    conv = _silu(_layernorm(cvbuf[...], cg_ref[...], cb_ref[...]))
    y_conv = jnp.dot(conv.astype(BF16), cwo_ref[...], preferred_element_type=F32)
    gates = jnp.dot(xb, w_in_ref[:, pw + 2 * cw:], preferred_element_type=F32)
    merged = (jax.nn.sigmoid(gates[:, :d_model]) * y_pool
              + jax.nn.sigmoid(gates[:, d_model:]) * y_conv)
    m = jnp.dot(merged.astype(BF16), w_out_ref[...], preferred_element_type=F32)
    x1 = _layernorm(alpha * x + m, g1_ref[...], b1_ref[...])
    x1_ref[...] = x1
    lt_ref[...] = lax.dot_general(wr_ref[...], x1.astype(BF16), (((1,), (1,)), ((), ())),
                                  preferred_element_type=F32)


def _mixer_call(x, w_in, pool_w, pool_scale, conv_dw, cln_g, cln_b, conv_w_out, w_out,
                ln1_g, ln1_b, w_router_t, *, alpha):
    b, s, d = x.shape
    ts = MIX_ROWS
    ns = s // ts
    pw = pool_w.shape[0] * pool_w.shape[1]
    cw = conv_dw.shape[1]
    n_exp = w_router_t.shape[0]

    def const(shape):
        return pl.BlockSpec(shape, lambda bi, si: (0,) * len(shape))

    return pl.pallas_call(
        functools.partial(_mixer_kernel, alpha=alpha),
        grid=(b, ns),
        in_specs=[
            pl.BlockSpec((1, ts, d), lambda bi, si: (bi, si, 0)),
            const(w_in.shape), const(pool_w.shape), const(pool_scale.shape), const(conv_dw.shape),
            const(cln_g.shape), const(cln_b.shape), const(conv_w_out.shape), const(w_out.shape),
            const(ln1_g.shape), const(ln1_b.shape), const(w_router_t.shape),
        ],
        out_specs=[
            pl.BlockSpec((ts, d), lambda bi, si: (bi * ns + si, 0)),
            pl.BlockSpec((n_exp, ts), lambda bi, si: (0, bi * ns + si)),
        ],
        out_shape=[jax.ShapeDtypeStruct((b * s, d), F32),
                   jax.ShapeDtypeStruct((n_exp, b * s), F32)],
        scratch_shapes=[
            pltpu.VMEM((POOL_HIST + ts, pw), F32),
            pltpu.VMEM((CONV_HIST + ts, cw), F32),
            pltpu.VMEM((ts, pw), BF16),
            pltpu.VMEM((ts, cw), F32),
        ],
        compiler_params=pltpu.CompilerParams(
            dimension_semantics=("arbitrary", "arbitrary"), vmem_limit_bytes=VMEM_LIMIT_MIXER),
        name="mixer",
    )(x, w_in, pool_w, pool_scale, conv_dw, cln_g, cln_b, conv_w_out, w_out, ln1_g, ln1_b,
      w_router_t)


ROUTE_BASE = 0
ROUTE_NCH = 1
ROUTE_CHUNK = 2
ROUTE_START = 3


def _route_kernel(lt_ref, bias_ref, w_ref, lpos_ref, bpos_ref, tab_ref, cnt_ref, carry):
    n_exp, tt = lt_ref.shape
    neg = -jnp.inf

    @pl.when(pl.program_id(0) == 0)
    def _():
        carry[...] = jnp.zeros_like(carry)

    scores = jax.nn.sigmoid(lt_ref[...])
    sel = scores + bias_ref[...]
    sel3 = sel.reshape(N_GROUPS, EXPERTS_PER_GROUP, tt)
    io3 = lax.broadcasted_iota(I32, sel3.shape, 1)
    m1 = jnp.max(sel3, axis=1, keepdims=True)
    i1 = jnp.min(jnp.where(sel3 == m1, io3, EXPERTS_PER_GROUP), axis=1, keepdims=True)
    m2 = jnp.max(jnp.where(io3 == i1, neg, sel3), axis=1, keepdims=True)
    gscore = m1 + m2
    iog = lax.broadcasted_iota(I32, gscore.shape, 0)
    gsel = jnp.zeros(gscore.shape, F32)
    for _ in range(TOPK_GROUPS):
        m = jnp.max(gscore, axis=0, keepdims=True)
        gi = jnp.min(jnp.where(gscore == m, iog, N_GROUPS), axis=0, keepdims=True)
        hit = iog == gi
        gsel = jnp.where(hit, 1.0, gsel)
        gscore = jnp.where(hit, neg, gscore)
    val = jnp.where(gsel > 0.5, sel3, neg).reshape(n_exp, tt)
    ioe = lax.broadcasted_iota(I32, (n_exp, tt), 0)
    member = jnp.zeros((n_exp, tt), F32)
    idxs, ws = [], []
    for _ in range(TOP_K):
        m = jnp.max(val, axis=0, keepdims=True)
        ei = jnp.min(jnp.where(val == m, ioe, n_exp), axis=0, keepdims=True)
        hit = ioe == ei
        idxs.append(ei)
        ws.append(jnp.sum(jnp.where(hit, scores, 0.0), axis=0, keepdims=True))
        member = jnp.where(hit, 1.0, member)
        val = jnp.where(hit, neg, val)
    w = jnp.concatenate(ws, axis=0)
    w_ref[...] = w / jnp.sum(w, axis=0, keepdims=True) * ROUTED_SCALE

    mb = member.astype(BF16)
    before = (lax.broadcasted_iota(I32, (tt, tt), 0) < lax.broadcasted_iota(I32, (tt, tt), 1))
    rank_in_tile = jnp.dot(mb, before.astype(BF16), preferred_element_type=F32)
    c_col = jnp.sum(member, axis=1, keepdims=True)
    c_row = lax.dot_general(jnp.ones((SUBLANES, tt), BF16), mb, (((1,), (1,)), ((), ())),
                            preferred_element_type=F32)

    def even(c):
        return c + (c - 2.0 * jnp.floor(c * 0.5))

    def chunks(c):
        return jnp.maximum(jnp.floor((c + (RUN_ROWS - 1)) * (1.0 / RUN_ROWS)), 1.0)

    ee0 = lax.broadcasted_iota(I32, (n_exp, n_exp), 0)
    ee1 = lax.broadcasted_iota(I32, (n_exp, n_exp), 1)
    lower = (ee1 < ee0).astype(BF16)
    upper = (ee0 < ee1).astype(BF16)

    def prefix_col(v):
        return jnp.dot(lower, jnp.broadcast_to(v, (n_exp, LANES)).astype(BF16),
                       preferred_element_type=F32)[:, 0:1]

    def prefix_row(v):
        return jnp.dot(v.astype(BF16), upper, preferred_element_type=F32)

    chunk_col = prefix_col(chunks(c_col))
    start_col = prefix_col(even(c_col))
    bfull = chunk_col * RUN_ROWS + rank_in_tile
    lfull = start_col + rank_in_tile
    bpos_ref[...] = jnp.concatenate(
        [jnp.sum(jnp.where(ioe == ei, bfull, 0.0), axis=0, keepdims=True) for ei in idxs],
        axis=0).astype(I32)
    lpos_ref[...] = jnp.concatenate(
        [jnp.sum(jnp.where(ioe == ei, lfull, 0.0), axis=0, keepdims=True) for ei in idxs],
        axis=0).astype(I32)
    nch_row = chunks(c_row)
    tab_ref[0] = jnp.concatenate(
        [carry[0:1], nch_row[0:1], prefix_row(nch_row)[0:1], prefix_row(even(c_row))[0:1],
         jnp.zeros((SUBLANES - 4, n_exp), F32)], axis=0)
    carry[...] = carry[...] + even(c_row)
    cnt_ref[...] = carry[...]


def _route_call(logits_t, bias_col):
    n_exp, t = logits_t.shape
    tt = TILE
    return pl.pallas_call(
        _route_kernel,
        grid=(t // tt,),
        in_specs=[pl.BlockSpec((n_exp, tt), lambda i: (0, i)),
                  pl.BlockSpec((n_exp, 1), lambda i: (0, 0))],
        out_specs=[pl.BlockSpec((TOP_K, tt), lambda i: (0, i)),
                   pl.BlockSpec((TOP_K, tt), lambda i: (0, i)),
                   pl.BlockSpec((TOP_K, tt), lambda i: (0, i)),
                   pl.BlockSpec((1, SUBLANES, n_exp), lambda i: (i, 0, 0)),
                   pl.BlockSpec((SUBLANES, n_exp), lambda i: (0, 0))],
        out_shape=[jax.ShapeDtypeStruct((TOP_K, t), F32),
                   jax.ShapeDtypeStruct((TOP_K, t), I32),
                   jax.ShapeDtypeStruct((TOP_K, t), I32),
                   jax.ShapeDtypeStruct((t // tt, SUBLANES, n_exp), F32),
                   jax.ShapeDtypeStruct((SUBLANES, n_exp), F32)],
        scratch_shapes=[pltpu.VMEM((SUBLANES, n_exp), F32)],
        compiler_params=pltpu.CompilerParams(
            dimension_semantics=("arbitrary",), vmem_limit_bytes=VMEM_LIMIT_OTHER),
        name="route",
    )(logits_t, bias_col)


TAB_SLOT = 0
TAB_NCH = 1
TAB_SRC = 2
TAB_META = 3


def _start_chunks(tab_ref, n_exp, src_step, make_copy):
    def first(h, c):
        for u in range(2):
            e = 2 * h + u
            make_copy(tab_ref[0, TAB_SRC, e], tab_ref[0, TAB_SLOT, e]).start(priority=u)
        return c

    lax.fori_loop(0, n_exp // 2, first, 0, unroll=4)

    @pl.when(tab_ref[0, TAB_META, 1] > 1)
    def _():
        def extra(e, c):
            def one(i, c2):
                make_copy(tab_ref[0, TAB_SRC, e] + i * src_step,
                          tab_ref[0, TAB_SLOT, e] + i * RUN_ROWS).start()
                return c2
            lax.fori_loop(1, tab_ref[0, TAB_NCH, e], one, 0)
            return c
        lax.fori_loop(0, n_exp, extra, 0)


def _wait_chunks(tab_ref, make_copy):
    def wait(i, c):
        make_copy(0, 0).wait()
        return c
    lax.fori_loop(0, tab_ref[0, TAB_META, 0], wait, 0)


def _dispatch_kernel(zflag_ref, tab_ref, prev_tab_ref, lpos_ref, x1_ref, xs_hbm,
                     sbuf, zbuf, zsem, sem):
    tt = x1_ref.shape[0]
    n_blocks = zflag_ref.shape[0]
    n_exp = tab_ref.shape[2]
    sub = zbuf.shape[0] // EXPERT_BLOCK
    chunk_rows = RUN_ROWS * sub
    i = pl.program_id(0)

    def zero_copy(blk):
        return pltpu.make_async_copy(
            zbuf, xs_hbm.at[pl.ds(pl.multiple_of(blk * zbuf.shape[0], zbuf.shape[0]),
                                  zbuf.shape[0])], zsem)

    @pl.when(i == 0)
    def _():
        zbuf[...] = jnp.zeros_like(zbuf)

        def start(blk, c):
            @pl.when(zflag_ref[blk] == 1)
            def _():
                zero_copy(blk).start()
            return c

        def wait(blk, c):
            @pl.when(zflag_ref[blk] == 1)
            def _():
                zero_copy(blk).wait()
            return c

        lax.fori_loop(0, n_blocks, start, 0)
        lax.fori_loop(0, n_blocks, wait, 0)

    def make_copy_from(buf):
        def make_copy(row, slot):
            return pltpu.make_async_copy(
                buf.at[pl.ds(pl.multiple_of(row * sub, 2 * sub), chunk_rows)],
                xs_hbm.at[pl.ds(pl.multiple_of(slot * sub, 2 * sub), chunk_rows)], sem)
        return make_copy

    xb = x1_ref[...].astype(BF16)

    def sort_into(buf):
        for rb in range(SORT_ROWS // SORT_BLOCK):
            ior = lax.broadcasted_iota(I32, (SORT_BLOCK, tt), 0) + rb * SORT_BLOCK
            p = jnp.zeros((SORT_BLOCK, tt), F32)
            for k in range(TOP_K):
                p = jnp.where(ior == lpos_ref[k:k + 1, :], 1.0, p)
            _store_packed_rows(buf, rb * SORT_BLOCK,
                               jnp.dot(p.astype(BF16), xb, preferred_element_type=F32))

    for s in range(2):
        @pl.when(i % 2 == s)
        def _():
            sort_into(sbuf.at[s])

    @pl.when(i > 0)
    def _():
        _wait_chunks(prev_tab_ref, make_copy_from(sbuf.at[0]))

    for s in range(2):
        @pl.when(i % 2 == s)
        def _():
            _start_chunks(tab_ref, n_exp, RUN_ROWS, make_copy_from(sbuf.at[s]))

    @pl.when(i == pl.num_programs(0) - 1)
    def _():
        _wait_chunks(tab_ref, make_copy_from(sbuf.at[0]))


def _dispatch_call(zflag, tab, lpos, x1, n_slots):
    t, d = x1.shape
    tt = TILE
    n_exp = tab.shape[2]
    sub = d // 2 // LANES
    tab_spec = lambda f: pl.BlockSpec((1, 4, n_exp), f, memory_space=pltpu.SMEM)
    grid_spec = pltpu.PrefetchScalarGridSpec(
        num_scalar_prefetch=1,
        grid=(t // tt,),
        in_specs=[tab_spec(lambda i, zf: (i, 0, 0)),
                  tab_spec(lambda i, zf: (jnp.maximum(i - 1, 0), 0, 0)),
                  pl.BlockSpec((TOP_K, tt), lambda i, zf: (0, i)),
                  pl.BlockSpec((tt, d), lambda i, zf: (i, 0))],
        out_specs=pl.BlockSpec(memory_space=pl.ANY),
        scratch_shapes=[pltpu.VMEM((2, SORT_ROWS * sub, LANES), U32),
                        pltpu.VMEM((EXPERT_BLOCK * sub, LANES), U32),
                        pltpu.SemaphoreType.DMA(()),
                        pltpu.SemaphoreType.DMA(())],
    )
    return pl.pallas_call(
        _dispatch_kernel,
        grid_spec=grid_spec,
        out_shape=jax.ShapeDtypeStruct((n_slots * sub, LANES), U32),
        compiler_params=pltpu.CompilerParams(
            dimension_semantics=("arbitrary",), vmem_limit_bytes=VMEM_LIMIT_OTHER),
        name="dispatch",
    )(zflag, tab, tab, lpos, x1)


def _expert_kernel(first_ref, nblk_ref, tail_ref, xs_hbm, wg_ref, wu_ref, wd_ref, ys_hbm,
                   wgu_s, wd_s, xbuf, ybuf, isem, osem):
    e = pl.program_id(0)
    last = pl.num_programs(0) - 1
    hid = wg_ref.shape[2]
    blk_rows = xbuf.shape[0] // EXPERT_RING
    sub = blk_rows // EXPERT_BLOCK
    n_used = tail_ref[0]

    def rows_of(blk):
        return pl.ds(pl.multiple_of(blk * blk_rows, blk_rows), blk_rows)

    def in_copy(g):
        s = g % EXPERT_RING
        return pltpu.make_async_copy(xs_hbm.at[rows_of(g)], xbuf.at[rows_of(s)], isem.at[s])

    def out_copy(g):
        s = g % EXPERT_RING
        return pltpu.make_async_copy(ybuf.at[rows_of(s)], ys_hbm.at[rows_of(g)], osem.at[s])

    @pl.when(e == 0)
    def _():
        for g in range(EXPERT_RING):
            in_copy(g).start()

    wgu_s[:, 0:hid] = wg_ref[0].astype(BF16)
    wgu_s[:, hid:2 * hid] = wu_ref[0].astype(BF16)
    wd_s[...] = wd_ref[0].astype(BF16)

    def consume(g0, m):
        gs = [g0 + j for j in range(m)]
        for g in gs:
            in_copy(g).wait()

            @pl.when(g >= EXPERT_RING)
            def _():
                out_copy(g).wait()
        row0 = [(g % EXPERT_RING) * EXPERT_BLOCK for g in gs]
        xb = jnp.concatenate([_load_packed_rows(xbuf, r, EXPERT_BLOCK, sub) for r in row0], axis=0)
        gu = jnp.dot(xb, wgu_s[...], preferred_element_type=F32)
        h = _silu(gu[:, :hid]) * gu[:, hid:]
        y = jnp.dot(h.astype(BF16), wd_s[...], preferred_element_type=F32)
        for j, g in enumerate(gs):
            _store_packed_rows(ybuf, row0[j], y[j * EXPERT_BLOCK:(j + 1) * EXPERT_BLOCK])
            out_copy(g).start()

            @pl.when(g + EXPERT_RING < n_used)
            def _():
                in_copy(g + EXPERT_RING).start()

    first = first_ref[e]
    n = nblk_ref[e]

    def pair(p, c):
        consume(first + 2 * p, 2)
        return c

    lax.fori_loop(0, n // 2, pair, 0)

    @pl.when(n % 2 == 1)
    def _():
        consume(first + n - 1, 1)

    @pl.when(e == last)
    def _():
        for g in range(EXPERT_RING):
            out_copy(g).wait()
        ybuf[0:blk_rows, :] = jnp.zeros((blk_rows, ybuf.shape[1]), ybuf.dtype)

        def start(b, c):
            pltpu.make_async_copy(ybuf.at[rows_of(0)], ys_hbm.at[rows_of(b)], osem.at[0]).start()
            return c

        def wait(b, c):
            pltpu.make_async_copy(ybuf.at[rows_of(0)], ys_hbm.at[rows_of(b)], osem.at[0]).wait()
            return c

        lax.fori_loop(tail_ref[0], tail_ref[1], start, 0)
        lax.fori_loop(tail_ref[0], tail_ref[1], wait, 0)


def _expert_call(first_blk, n_blk, tail, xs, w_gate_e, w_up_e, w_down_e):
    n_exp, d, hid = w_gate_e.shape
    blk_rows = EXPERT_BLOCK * (d // 2 // LANES)
    grid_spec = pltpu.PrefetchScalarGridSpec(
        num_scalar_prefetch=3,
        grid=(n_exp,),
        in_specs=[pl.BlockSpec(memory_space=pl.ANY),
                  pl.BlockSpec((1, d, hid), lambda e, *_: (e, 0, 0)),
                  pl.BlockSpec((1, d, hid), lambda e, *_: (e, 0, 0)),
                  pl.BlockSpec((1, hid, d), lambda e, *_: (e, 0, 0))],
        out_specs=pl.BlockSpec(memory_space=pl.ANY),
        scratch_shapes=[pltpu.VMEM((d, 2 * hid), BF16), pltpu.VMEM((hid, d), BF16),
                        pltpu.VMEM((EXPERT_RING * blk_rows, LANES), U32),
                        pltpu.VMEM((EXPERT_RING * blk_rows, LANES), U32),
                        pltpu.SemaphoreType.DMA((EXPERT_RING,)),
                        pltpu.SemaphoreType.DMA((EXPERT_RING,))],
    )
    return pl.pallas_call(
        _expert_kernel,
        grid_spec=grid_spec,
        out_shape=jax.ShapeDtypeStruct(xs.shape, U32),
        compiler_params=pltpu.CompilerParams(
            dimension_semantics=("arbitrary",), vmem_limit_bytes=VMEM_LIMIT_OTHER),
        name="experts",
    )(first_blk, n_blk, tail, xs, w_gate_e, w_up_e, w_down_e)


def _combine_kernel(tab_ref, next_tab_ref, x1_ref, bpos_ref, w_ref, ys_hbm, wsgu_ref, wsd_ref,
                    g2_ref, b2_ref, out_ref, buf, acc, bp_s, wq_s, sem, *, alpha):
    tt = x1_ref.shape[0]
    hid = wsd_ref.shape[0]
    sub = buf.shape[0] // BUF_ROWS
    chunk_rows = RUN_ROWS * sub
    n_exp = tab_ref.shape[2]
    cpb = SORT_BLOCK // RUN_ROWS
    base_blocks = n_exp // cpb
    i = pl.program_id(0)

    def make_copy(chunk, slot):
        return pltpu.make_async_copy(
            ys_hbm.at[pl.ds(pl.multiple_of(slot * sub, 2 * sub), chunk_rows)],
            buf.at[pl.ds(pl.multiple_of(chunk * chunk_rows, chunk_rows), chunk_rows)],
            sem.at[chunk // cpb])

    def base_block_copies(kb):
        return pltpu.make_async_copy(
            ys_hbm.at[pl.ds(0, cpb * chunk_rows)],
            buf.at[pl.ds(kb * cpb * chunk_rows, cpb * chunk_rows)], sem.at[kb])

    @pl.when(i == 0)
    def _():
        buf[...] = jnp.zeros_like(buf)
        _start_chunks(tab_ref, n_exp, 1, make_copy)

    @pl.when((i > 0) & (tab_ref[0, TAB_META, 1] > 1))
    def _():
        for kb in range(base_blocks):
            base_block_copies(kb).wait()
        _start_chunks(tab_ref, n_exp, 1, make_copy)

    x1 = x1_ref[...]
    gu = jnp.dot(x1.astype(BF16), wsgu_ref[...], preferred_element_type=F32)
    hs = _silu(gu[:, :hid]) * gu[:, hid:]
    acc[...] = jnp.dot(hs.astype(BF16), wsd_ref[...], preferred_element_type=F32)

    for k in range(TOP_K):
        bp_s[k] = jnp.broadcast_to(bpos_ref[:, k:k + 1], (tt, LANES)).astype(I16)
        wq_s[k] = jnp.broadcast_to(w_ref[:, k:k + 1], (tt, LANES)).astype(BF16)

    def unsort_block(kb):
        qs = []
        for c0 in range(kb * SORT_BLOCK, (kb + 1) * SORT_BLOCK, LANES):
            iol = (lax.broadcasted_iota(I32, (tt, LANES), 1) + c0).astype(I16)
            q = jnp.zeros((tt, LANES), BF16)
            for k in range(TOP_K):
                q = jnp.where(iol == bp_s[k], wq_s[k], q)
            qs.append(q)
        rows = _load_packed_rows(buf, kb * SORT_BLOCK, SORT_BLOCK, sub)
        acc[...] += jnp.dot(jnp.concatenate(qs, axis=1), rows, preferred_element_type=F32)

    for kb in range(base_blocks):
        base_block_copies(kb).wait()
        unsort_block(kb)
        for u in range(cpb):
            e = kb * cpb + u
            make_copy(e, next_tab_ref[0, TAB_SLOT, e]).start(priority=u % 2)

    n_ch = tab_ref[0, TAB_META, 0]
    for kb in range(base_blocks, BUF_ROWS // SORT_BLOCK):
        @pl.when(kb * cpb < n_ch)
        def _():
            def wait(j, c):
                make_copy(kb * cpb, 0).wait()
                return c
            lax.fori_loop(0, jnp.minimum(n_ch - kb * cpb, cpb), wait, 0)
            unsort_block(kb)

    out_ref[...] = _layernorm(alpha * x1 + acc[...], g2_ref[...], b2_ref[...])

    @pl.when(i == pl.num_programs(0) - 1)
    def _():
        for kb in range(base_blocks):
            base_block_copies(kb).wait()


def _combine_call(tab, x1, bpos_tok, w_tok, ys, ws_gu, ws_d, ln2_g, ln2_b, *, alpha):
    t, d = x1.shape
    tt = TILE
    n_tiles = t // tt
    n_exp = tab.shape[2]

    def const(shape):
        return pl.BlockSpec(shape, lambda i: (0,) * len(shape))

    tab_spec = lambda f: pl.BlockSpec((1, 4, n_exp), f, memory_space=pltpu.SMEM)
    return pl.pallas_call(
        functools.partial(_combine_kernel, alpha=alpha),
        grid=(n_tiles,),
        in_specs=[tab_spec(lambda i: (i, 0, 0)),
                  tab_spec(lambda i: (jnp.minimum(i + 1, n_tiles - 1), 0, 0)),
                  pl.BlockSpec((tt, d), lambda i: (i, 0)),
                  pl.BlockSpec((tt, TOP_K), lambda i: (i, 0)),
                  pl.BlockSpec((tt, TOP_K), lambda i: (i, 0)),
                  pl.BlockSpec(memory_space=pl.ANY),
                  const(ws_gu.shape), const(ws_d.shape), const(ln2_g.shape), const(ln2_b.shape)],
        out_specs=pl.BlockSpec((tt, d), lambda i: (i, 0)),
        out_shape=jax.ShapeDtypeStruct((t, d), F32),
        scratch_shapes=[pltpu.VMEM((BUF_ROWS * (d // 2 // LANES), LANES), U32),
                        pltpu.VMEM((tt, d), F32),
                        pltpu.VMEM((TOP_K, tt, LANES), I16),
                        pltpu.VMEM((TOP_K, tt, LANES), BF16),
                        pltpu.SemaphoreType.DMA((BUF_ROWS // SORT_BLOCK,))],
        compiler_params=pltpu.CompilerParams(
            dimension_semantics=("arbitrary",), vmem_limit_bytes=VMEM_LIMIT_OTHER),
        name="combine",
    )(tab, tab, x1, bpos_tok, w_tok, ys, ws_gu, ws_d, ln2_g, ln2_b)


def _layer(x, w_in, pool_w, pool_scale, conv_dw, conv_ln_g, conv_ln_b, conv_w_out, w_out,
           ln1_g, ln1_b, w_router, router_bias, w_gate_e, w_up_e, w_down_e,
           ws_gate, ws_up, ws_down, ln2_g, ln2_b, *, alpha):
    b, s, d = x.shape
    t = b * s
    row = lambda v: v.reshape(1, -1)
    x1, logits_t = _mixer_call(
        x, w_in.astype(BF16), pool_w.astype(BF16), row(pool_scale), conv_dw, row(conv_ln_g),
        row(conv_ln_b), conv_w_out.astype(BF16), w_out.astype(BF16), row(ln1_g), row(ln1_b),
        w_router.T.astype(BF16), alpha=alpha)
    w_t, lpos_t, bpos_t, rtab, counts = _route_call(logits_t, router_bias.reshape(-1, 1))

    n = t * TOP_K
    n_blocks = -(-(n + (t // TILE) * N_EXPERTS + N_EXPERTS * (EXPERT_BLOCK - 1 + RUN_ROWS))
                 // EXPERT_BLOCK)
    n_slots = n_blocks * EXPERT_BLOCK
    counts = counts[0].astype(I32)
    n_blk = (counts + RUN_ROWS + EXPERT_BLOCK - 1) // EXPERT_BLOCK
    end_blk = jnp.cumsum(n_blk)
    first_blk = end_blk - n_blk
    pad_start = first_blk * EXPERT_BLOCK
    n_used = end_blk[-1]
    blk = jnp.arange(n_blocks, dtype=I32)
    block_e = jnp.minimum(jnp.sum((end_blk[None, :] <= blk[:, None]).astype(I32), axis=1),
                          N_EXPERTS - 1)
    real_end = (pad_start + counts)[block_e]
    zflag = ((blk >= n_used) | ((blk + 1) * EXPERT_BLOCK > real_end)).astype(I32)
    tail = jnp.stack([n_used, jnp.asarray(n_blocks, I32)])

    rtab = rtab.astype(I32)
    nch = rtab[:, ROUTE_NCH, :]
    meta = jnp.zeros_like(nch).at[:, 0].set(jnp.sum(nch, axis=1)).at[:, 1].set(jnp.max(nch, axis=1))
    slot = rtab[:, ROUTE_BASE, :] + pad_start[None, :]
    tab_d = jnp.stack([slot, nch, rtab[:, ROUTE_START, :], meta], axis=1)
    tab_c = jnp.stack([slot, nch, rtab[:, ROUTE_CHUNK, :], meta], axis=1)

    xs = _dispatch_call(zflag, tab_d, lpos_t, x1, n_slots)
    ys = _expert_call(first_blk, n_blk, tail, xs, w_gate_e, w_up_e, w_down_e)
    ws_gu = jnp.concatenate([ws_gate, ws_up], axis=1).astype(BF16)
    out = _combine_call(tab_c, x1, bpos_t.T, w_t.T, ys, ws_gu, ws_down.astype(BF16), row(ln2_g),
                        row(ln2_b), alpha=alpha)
    return out.reshape(b, s, d)


def kernel(x, w_in, pool_w, pool_scale, conv_dw, conv_ln_g, conv_ln_b, conv_w_out, w_out, ln1_g, ln1_b, w_router, router_bias, w_gate_e, w_up_e, w_down_e, ws_gate, ws_up, ws_down, ln2_g, ln2_b):
    depth = w_in.shape[0]
    alpha = (2.0 * depth) ** 0.25
    for l in range(depth):
        x = _layer(x, w_in[l], pool_w[l], pool_scale[l], conv_dw[l], conv_ln_g[l], conv_ln_b[l],
                   conv_w_out[l], w_out[l], ln1_g[l], ln1_b[l], w_router[l], router_bias[l],
                   w_gate_e[l], w_up_e[l], w_down_e[l], ws_gate[l], ws_up[l], ws_down[l],
                   ln2_g[l], ln2_b[l], alpha=alpha)
    return x
```

```python
import functools

import jax
import jax.numpy as jnp
from jax import lax
from jax.experimental import pallas as pl
from jax.experimental.pallas import tpu as pltpu

F32 = jnp.float32
BF16 = jnp.bfloat16
I32 = jnp.int32
U32 = jnp.uint32
I16 = jnp.int16

POOL_GROUPS = 4
POOL_WINDOWS = (2, 4, 8, 16)
CONV_KERNEL = 31
N_EXPERTS = 256
TOP_K = 8
N_GROUPS = 8
TOPK_GROUPS = 4
EXPERTS_PER_GROUP = N_EXPERTS // N_GROUPS
ROUTED_SCALE = 2.5
LN_EPS = 1e-5

SUBLANES = 8
LANES = 128
MIX_ROWS = 512
POOL_CHUNK = 32
CONV_CHUNK = 64
CONV_COLS = 256
POOL_HIST = 16
CONV_HIST = 32
TILE = 512
RUN_ROWS = 32
SORT_BLOCK = 512
MAX_CHUNKS = N_EXPERTS + TILE * TOP_K // RUN_ROWS
BUF_ROWS = MAX_CHUNKS * RUN_ROWS
SORT_ROWS = -(-(TILE * TOP_K + N_EXPERTS + RUN_ROWS) // SORT_BLOCK) * SORT_BLOCK
EXPERT_BLOCK = 256
EXPERT_RING = 6
VMEM_LIMIT_MIXER = 56 * 1024 * 1024
VMEM_LIMIT_OTHER = 56 * 1024 * 1024


def _layernorm(z, g, b):
    mu = jnp.mean(z, axis=-1, keepdims=True)
    d = z - mu
    var = jnp.mean(d * d, axis=-1, keepdims=True)
    return d * lax.rsqrt(var + LN_EPS) * g + b


def _silu(v):
    return v * jax.nn.sigmoid(v)


def _store_packed_rows(ref, row0, v):
    n, d = v.shape
    half = d // 2
    sub = half // LANES
    for j in range(sub):
        words = pltpu.pack_elementwise(
            [v[:, j * LANES:(j + 1) * LANES], v[:, half + j * LANES:half + (j + 1) * LANES]],
            packed_dtype=BF16)
        ref[pl.ds(row0 * sub + j, n, stride=sub), :] = words


def _load_packed_rows(ref, row0, n, sub):
    los, his = [], []
    for j in range(sub):
        words = ref[pl.ds(row0 * sub + j, n, stride=sub), :]
        los.append(pltpu.unpack_elementwise(words, index=0, packed_dtype=BF16,
                                            unpacked_dtype=F32).astype(BF16))
        his.append(pltpu.unpack_elementwise(words, index=1, packed_dtype=BF16,
                                            unpacked_dtype=F32).astype(BF16))
    return jnp.concatenate(los + his, axis=1)


def _mixer_kernel(x_ref, w_in_ref, pool_w_ref, pool_scale_ref, dw_ref, cg_ref, cb_ref, cwo_ref,
                  w_out_ref, g1_ref, b1_ref, wr_ref, x1_ref, lt_ref,
                  ubuf, vbuf, rbuf, cvbuf, *, alpha):
    ts = x_ref.shape[1]
    d_model = x_ref.shape[2]
    pw = ubuf.shape[1]
    cw = vbuf.shape[1]
    gi = pw // POOL_GROUPS
    si = pl.program_id(1)

    @pl.when(si == 0)
    def _():
        ubuf[0:POOL_HIST, :] = jnp.zeros((POOL_HIST, pw), F32)
        vbuf[0:CONV_HIST, :] = jnp.zeros((CONV_HIST, cw), F32)

    x = x_ref[0]
    xb = x.astype(BF16)
    ubuf[POOL_HIST:POOL_HIST + ts, :] = jnp.dot(xb, w_in_ref[:, 0:pw], preferred_element_type=F32)
    a = jnp.dot(xb, w_in_ref[:, pw:pw + 2 * cw], preferred_element_type=F32)
    vbuf[CONV_HIST:CONV_HIST + ts, :] = a[:, :cw] * jax.nn.sigmoid(a[:, cw:])
    pos0 = si * ts

    for c in range(ts // POOL_CHUNK):
        r0 = c * POOL_CHUNK
        pos = pos0 + r0 + lax.broadcasted_iota(I32, (POOL_CHUNK, 1), 0)
        for g, w in enumerate(POOL_WINDOWS):
            cols = slice(g * gi, (g + 1) * gi)
            cur = ubuf[pl.ds(POOL_HIST + r0, POOL_CHUNK), cols]
            s = cur
            for j in range(1, w):
                s = s + ubuf[pl.ds(POOL_HIST + r0 - j, POOL_CHUNK), cols]
            cnt = jnp.minimum(pos + 1, w).astype(F32)
            rbuf[pl.ds(r0, POOL_CHUNK), cols] = (s / cnt - cur).astype(BF16)

    for c in range(ts // CONV_CHUNK):
        r0 = c * CONV_CHUNK
        base = CONV_HIST + r0 - (CONV_KERNEL - 1)
        for c0 in range(0, cw, CONV_COLS):
            cols = slice(c0, c0 + CONV_COLS)
            parts = []
            for rho in range(SUBLANES):
                taps = range(rho, CONV_KERNEL, SUBLANES)
                win = vbuf[pl.ds(base + rho, CONV_CHUNK + taps[-1] - rho), cols]
                part = None
                for k in taps:
                    term = dw_ref[k:k + 1, cols] * win[k - rho:k - rho + CONV_CHUNK, :]
                    part = term if part is None else part + term
                parts.append(part)
            cvbuf[pl.ds(r0, CONV_CHUNK), cols] = functools.reduce(lambda p, q: p + q, parts)

    ubuf[0:POOL_HIST, :] = ubuf[ts:ts + POOL_HIST, :]
    vbuf[0:CONV_HIST, :] = vbuf[ts:ts + CONV_HIST, :]

    y_pool = jnp.concatenate(
        [jnp.dot(rbuf[:, g * gi:(g + 1) * gi], pool_w_ref[g], preferred_element_type=F32)
         for g in range(POOL_GROUPS)], axis=1) * pool_scale_ref[...]
    conv = _silu(_layernorm(cvbuf[...], cg_ref[...], cb_ref[...]))
    y_conv = jnp.dot(conv.astype(BF16), cwo_ref[...], preferred_element_type=F32)
    gates = jnp.dot(xb, w_in_ref[:, pw + 2 * cw:], preferred_element_type=F32)
    merged = (jax.nn.sigmoid(gates[:, :d_model]) * y_pool
              + jax.nn.sigmoid(gates[:, d_model:]) * y_conv)
    m = jnp.dot(merged.astype(BF16), w_out_ref[...], preferred_element_type=F32)
    x1 = _layernorm(alpha * x + m, g1_ref[...], b1_ref[...])
    x1_ref[...] = x1
    lt_ref[...] = lax.dot_general(wr_ref[...], x1.astype(BF16), (((1,), (1,)), ((), ())),
                                  preferred_element_type=F32)


def _mixer_call(x, w_in, pool_w, pool_scale, conv_dw, cln_g, cln_b, conv_w_out, w_out,
                ln1_g, ln1_b, w_router_t, *, alpha):
    b, s, d = x.shape
    ts = MIX_ROWS
    ns = s // ts
    pw = pool_w.shape[0] * pool_w.shape[1]
    cw = conv_dw.shape[1]
    n_exp = w_router_t.shape[0]

    def const(shape):
        return pl.BlockSpec(shape, lambda bi, si: (0,) * len(shape))

    return pl.pallas_call(
        functools.partial(_mixer_kernel, alpha=alpha),
        grid=(b, ns),
        in_specs=[
            pl.BlockSpec((1, ts, d), lambda bi, si: (bi, si, 0)),
            const(w_in.shape), const(pool_w.shape), const(pool_scale.shape), const(conv_dw.shape),
            const(cln_g.shape), const(cln_b.shape), const(conv_w_out.shape), const(w_out.shape),
            const(ln1_g.shape), const(ln1_b.shape), const(w_router_t.shape),
        ],
        out_specs=[
            pl.BlockSpec((ts, d), lambda bi, si: (bi * ns + si, 0)),
            pl.BlockSpec((n_exp, ts), lambda bi, si: (0, bi * ns + si)),
        ],
        out_shape=[jax.ShapeDtypeStruct((b * s, d), F32),
                   jax.ShapeDtypeStruct((n_exp, b * s), F32)],
        scratch_shapes=[
            pltpu.VMEM((POOL_HIST + ts, pw), F32),
            pltpu.VMEM((CONV_HIST + ts, cw), F32),
            pltpu.VMEM((ts, pw), BF16),
            pltpu.VMEM((ts, cw), F32),
        ],
        compiler_params=pltpu.CompilerParams(
            dimension_semantics=("arbitrary", "arbitrary"), vmem_limit_bytes=VMEM_LIMIT_MIXER),
        name="mixer",
    )(x, w_in, pool_w, pool_scale, conv_dw, cln_g, cln_b, conv_w_out, w_out, ln1_g, ln1_b,
      w_router_t)


ROUTE_BASE = 0
ROUTE_NCH = 1
ROUTE_CHUNK = 2
ROUTE_START = 3
ROUTE_LEN = 4


def _route_kernel(lt_ref, bias_ref, w_ref, lpos_ref, bpos_ref, tab_ref, cnt_ref, carry):
    n_exp, tt = lt_ref.shape
    neg = -jnp.inf

    @pl.when(pl.program_id(0) == 0)
    def _():
        carry[...] = jnp.zeros_like(carry)

    scores = jax.nn.sigmoid(lt_ref[...])
    sel = scores + bias_ref[...]
    sel3 = sel.reshape(N_GROUPS, EXPERTS_PER_GROUP, tt)
    io3 = lax.broadcasted_iota(I32, sel3.shape, 1)
    m1 = jnp.max(sel3, axis=1, keepdims=True)
    i1 = jnp.min(jnp.where(sel3 == m1, io3, EXPERTS_PER_GROUP), axis=1, keepdims=True)
    m2 = jnp.max(jnp.where(io3 == i1, neg, sel3), axis=1, keepdims=True)
    gscore = m1 + m2
    iog = lax.broadcasted_iota(I32, gscore.shape, 0)
    gsel = jnp.zeros(gscore.shape, F32)
    for _ in range(TOPK_GROUPS):
        m = jnp.max(gscore, axis=0, keepdims=True)
        gi = jnp.min(jnp.where(gscore == m, iog, N_GROUPS), axis=0, keepdims=True)
        hit = iog == gi
        gsel = jnp.where(hit, 1.0, gsel)
        gscore = jnp.where(hit, neg, gscore)
    val = jnp.where(gsel > 0.5, sel3, neg).reshape(n_exp, tt)
    ioe = lax.broadcasted_iota(I32, (n_exp, tt), 0)
    member = jnp.zeros((n_exp, tt), F32)
    idxs, ws = [], []
    for _ in range(TOP_K):
        m = jnp.max(val, axis=0, keepdims=True)
        ei = jnp.min(jnp.where(val == m, ioe, n_exp), axis=0, keepdims=True)
        hit = ioe == ei
        idxs.append(ei)
        ws.append(jnp.sum(jnp.where(hit, scores, 0.0), axis=0, keepdims=True))
        member = jnp.where(hit, 1.0, member)
        val = jnp.where(hit, neg, val)
    w = jnp.concatenate(ws, axis=0)
    w_ref[...] = w / jnp.sum(w, axis=0, keepdims=True) * ROUTED_SCALE

    mb = member.astype(BF16)
    before = (lax.broadcasted_iota(I32, (tt, tt), 0) < lax.broadcasted_iota(I32, (tt, tt), 1))
    rank_in_tile = jnp.dot(mb, before.astype(BF16), preferred_element_type=F32)
    c_col = jnp.sum(member, axis=1, keepdims=True)
    c_row = lax.dot_general(jnp.ones((SUBLANES, tt), BF16), mb, (((1,), (1,)), ((), ())),
                            preferred_element_type=F32)

    def even(c):
        return c + (c - 2.0 * jnp.floor(c * 0.5))

    def chunks(c):
        return jnp.maximum(jnp.floor((c + (RUN_ROWS - 1)) * (1.0 / RUN_ROWS)), 1.0)

    ee0 = lax.broadcasted_iota(I32, (n_exp, n_exp), 0)
    ee1 = lax.broadcasted_iota(I32, (n_exp, n_exp), 1)
    lower = (ee1 < ee0).astype(BF16)
    upper = (ee0 < ee1).astype(BF16)

    def prefix_col(v):
        return jnp.dot(lower, jnp.broadcast_to(v, (n_exp, LANES)).astype(BF16),
                       preferred_element_type=F32)[:, 0:1]

    def prefix_row(v):
        return jnp.dot(v.astype(BF16), upper, preferred_element_type=F32)

    chunk_col = prefix_col(chunks(c_col))
    start_col = prefix_col(even(c_col))
    bfull = chunk_col * RUN_ROWS + rank_in_tile
    lfull = start_col + rank_in_tile
    bpos_ref[...] = jnp.concatenate(
        [jnp.sum(jnp.where(ioe == ei, bfull, 0.0), axis=0, keepdims=True) for ei in idxs],
        axis=0).astype(I32)
    lpos_ref[...] = jnp.concatenate(
        [jnp.sum(jnp.where(ioe == ei, lfull, 0.0), axis=0, keepdims=True) for ei in idxs],
        axis=0).astype(I32)
    nch_row = chunks(c_row)
    tab_ref[0] = jnp.concatenate(
        [carry[0:1], nch_row[0:1], prefix_row(nch_row)[0:1], prefix_row(even(c_row))[0:1],
         even(c_row)[0:1], jnp.zeros((SUBLANES - 5, n_exp), F32)], axis=0)
    carry[...] = carry[...] + even(c_row)
    cnt_ref[...] = carry[...]


def _route_call(logits_t, bias_col):
    n_exp, t = logits_t.shape
    tt = TILE
    return pl.pallas_call(
        _route_kernel,
        grid=(t // tt,),
        in_specs=[pl.BlockSpec((n_exp, tt), lambda i: (0, i)),
                  pl.BlockSpec((n_exp, 1), lambda i: (0, 0))],
        out_specs=[pl.BlockSpec((TOP_K, tt), lambda i: (0, i)),
                   pl.BlockSpec((TOP_K, tt), lambda i: (0, i)),
                   pl.BlockSpec((TOP_K, tt), lambda i: (0, i)),
                   pl.BlockSpec((1, SUBLANES, n_exp), lambda i: (i, 0, 0)),
                   pl.BlockSpec((SUBLANES, n_exp), lambda i: (0, 0))],
        out_shape=[jax.ShapeDtypeStruct((TOP_K, t), F32),
                   jax.ShapeDtypeStruct((TOP_K, t), I32),
                   jax.ShapeDtypeStruct((TOP_K, t), I32),
                   jax.ShapeDtypeStruct((t // tt, SUBLANES, n_exp), F32),
                   jax.ShapeDtypeStruct((SUBLANES, n_exp), F32)],
        scratch_shapes=[pltpu.VMEM((SUBLANES, n_exp), F32)],
        compiler_params=pltpu.CompilerParams(
            dimension_semantics=("arbitrary",), vmem_limit_bytes=VMEM_LIMIT_OTHER),
        name="route",
    )(logits_t, bias_col)


TAB_SLOT = 0
TAB_NCH = 1
TAB_SRC = 2
TAB_META = 3
TAB_LEN = 4


def _start_chunks(tab_ref, n_exp, src_step, make_copy):
    def first(h, c):
        for u in range(2):
            e = 2 * h + u
            make_copy(tab_ref[0, TAB_SRC, e], tab_ref[0, TAB_SLOT, e]).start(priority=u)
        return c

    lax.fori_loop(0, n_exp // 2, first, 0, unroll=4)

    @pl.when(tab_ref[0, TAB_META, 1] > 1)
    def _():
        def extra(e, c):
            def one(i, c2):
                make_copy(tab_ref[0, TAB_SRC, e] + i * src_step,
                          tab_ref[0, TAB_SLOT, e] + i * RUN_ROWS).start()
                return c2
            lax.fori_loop(1, tab_ref[0, TAB_NCH, e], one, 0)
            return c
        lax.fori_loop(0, n_exp, extra, 0)


def _wait_chunks(tab_ref, make_copy):
    def wait(i, c):
        make_copy(0, 0).wait()
        return c
    lax.fori_loop(0, tab_ref[0, TAB_META, 0], wait, 0)


def _dispatch_kernel(zflag_ref, tab_ref, prev_tab_ref, lpos_ref, x1_ref, xs_hbm,
                     sbuf, zbuf, zsem, sem):
    tt = x1_ref.shape[0]
    n_blocks = zflag_ref.shape[0]
    n_exp = tab_ref.shape[2]
    sub = zbuf.shape[0] // EXPERT_BLOCK
    chunk_rows = RUN_ROWS * sub
    i = pl.program_id(0)

    def zero_copy(blk):
        return pltpu.make_async_copy(
            zbuf, xs_hbm.at[pl.ds(pl.multiple_of(blk * zbuf.shape[0], zbuf.shape[0]),
                                  zbuf.shape[0])], zsem)

    @pl.when(i == 0)
    def _():
        zbuf[...] = jnp.zeros_like(zbuf)

        def start(blk, c):
            @pl.when(zflag_ref[blk] == 1)
            def _():
                zero_copy(blk).start()
            return c

        def wait(blk, c):
            @pl.when(zflag_ref[blk] == 1)
            def _():
                zero_copy(blk).wait()
            return c

        lax.fori_loop(0, n_blocks, start, 0)
        lax.fori_loop(0, n_blocks, wait, 0)

    half_rows = chunk_rows // 2

    def make_copy_from(buf, n_rows):
        def make_copy(row, slot):
            return pltpu.make_async_copy(
                buf.at[pl.ds(pl.multiple_of(row * sub, 2 * sub), n_rows)],
                xs_hbm.at[pl.ds(pl.multiple_of(slot * sub, 2 * sub), n_rows)], sem)
        return make_copy

    def start_copies(buf):
        full = make_copy_from(buf, chunk_rows)
        half = make_copy_from(buf, half_rows)

        def first(h, c):
            for u in range(2):
                e = 2 * h + u
                src, slot = tab_ref[0, TAB_SRC, e], tab_ref[0, TAB_SLOT, e]
                short = tab_ref[0, TAB_LEN, e] <= RUN_ROWS // 2

                @pl.when(short)
                def _():
                    half(src, slot).start(priority=u)

                @pl.when(jnp.logical_not(short))
                def _():
                    full(src, slot).start(priority=u)
            return c

        lax.fori_loop(0, n_exp // 2, first, 0, unroll=2)

        @pl.when(tab_ref[0, TAB_META, 1] > 1)
        def _():
            def extra(e, c):
                def one(j, c2):
                    full(tab_ref[0, TAB_SRC, e] + j * RUN_ROWS,
                         tab_ref[0, TAB_SLOT, e] + j * RUN_ROWS).start()
                    return c2
                lax.fori_loop(1, tab_ref[0, TAB_NCH, e], one, 0)
                return c
            lax.fori_loop(0, n_exp, extra, 0)

    def wait_copies(tab):
        def wait(j, c):
            make_copy_from(sbuf.at[0], half_rows)(0, 0).wait()
            return c
        lax.fori_loop(0, tab[0, TAB_META, 0], wait, 0)

    xb = x1_ref[...].astype(BF16)

    def sort_into(buf):
        for rb in range(SORT_ROWS // SORT_BLOCK):
            ior = lax.broadcasted_iota(I32, (SORT_BLOCK, tt), 0) + rb * SORT_BLOCK
            p = jnp.zeros((SORT_BLOCK, tt), F32)
            for k in range(TOP_K):
                p = jnp.where(ior == lpos_ref[k:k + 1, :], 1.0, p)
            _store_packed_rows(buf, rb * SORT_BLOCK,
                               jnp.dot(p.astype(BF16), xb, preferred_element_type=F32))

    for s in range(2):
        @pl.when(i % 2 == s)
        def _():
            sort_into(sbuf.at[s])

    @pl.when(i > 0)
    def _():
        wait_copies(prev_tab_ref)

    for s in range(2):
        @pl.when(i % 2 == s)
        def _():
            start_copies(sbuf.at[s])

    @pl.when(i == pl.num_programs(0) - 1)
    def _():
        wait_copies(tab_ref)


def _dispatch_call(zflag, tab, lpos, x1, n_slots):
    t, d = x1.shape
    tt = TILE
    n_exp = tab.shape[2]
    sub = d // 2 // LANES
    tab_spec = lambda f: pl.BlockSpec((1, tab.shape[1], n_exp), f, memory_space=pltpu.SMEM)
    grid_spec = pltpu.PrefetchScalarGridSpec(
        num_scalar_prefetch=1,
        grid=(t // tt,),
        in_specs=[tab_spec(lambda i, zf: (i, 0, 0)),
                  tab_spec(lambda i, zf: (jnp.maximum(i - 1, 0), 0, 0)),
                  pl.BlockSpec((TOP_K, tt), lambda i, zf: (0, i)),
                  pl.BlockSpec((tt, d), lambda i, zf: (i, 0))],
        out_specs=pl.BlockSpec(memory_space=pl.ANY),
        scratch_shapes=[pltpu.VMEM((2, SORT_ROWS * sub, LANES), U32),
                        pltpu.VMEM((EXPERT_BLOCK * sub, LANES), U32),
                        pltpu.SemaphoreType.DMA(()),
                        pltpu.SemaphoreType.DMA(())],
    )
    return pl.pallas_call(
        _dispatch_kernel,
        grid_spec=grid_spec,
        out_shape=jax.ShapeDtypeStruct((n_slots * sub, LANES), U32),
        compiler_params=pltpu.CompilerParams(
            dimension_semantics=("arbitrary",), vmem_limit_bytes=VMEM_LIMIT_OTHER),
        name="dispatch",
    )(zflag, tab, tab, lpos, x1)


def _expert_kernel(first_ref, nblk_ref, tail_ref, xs_hbm, wg_ref, wu_ref, wd_ref, ys_hbm,
                   wgu_s, wd_s, xbuf, ybuf, isem, osem):
    e = pl.program_id(0)
    last = pl.num_programs(0) - 1
    hid = wg_ref.shape[2]
    blk_rows = xbuf.shape[0] // EXPERT_RING
    sub = blk_rows // EXPERT_BLOCK
    n_used = tail_ref[0]

    def rows_of(blk):
        return pl.ds(pl.multiple_of(blk * blk_rows, blk_rows), blk_rows)

    def in_copy(g):
        s = g % EXPERT_RING
        return pltpu.make_async_copy(xs_hbm.at[rows_of(g)], xbuf.at[rows_of(s)], isem.at[s])

    def out_copy(g):
        s = g % EXPERT_RING
        return pltpu.make_async_copy(ybuf.at[rows_of(s)], ys_hbm.at[rows_of(g)], osem.at[s])

    @pl.when(e == 0)
    def _():
        for g in range(EXPERT_RING):
            in_copy(g).start()

    wgu_s[:, 0:hid] = wg_ref[0].astype(BF16)
    wgu_s[:, hid:2 * hid] = wu_ref[0].astype(BF16)
    wd_s[...] = wd_ref[0].astype(BF16)

    def consume(g0, m):
        gs = [g0 + j for j in range(m)]
        for g in gs:
            in_copy(g).wait()

            @pl.when(g >= EXPERT_RING)
            def _():
                out_copy(g).wait()
        row0 = [(g % EXPERT_RING) * EXPERT_BLOCK for g in gs]
        xb = jnp.concatenate([_load_packed_rows(xbuf, r, EXPERT_BLOCK, sub) for r in row0], axis=0)
        gu = jnp.dot(xb, wgu_s[...], preferred_element_type=F32)
        h = _silu(gu[:, :hid]) * gu[:, hid:]
        y = jnp.dot(h.astype(BF16), wd_s[...], preferred_element_type=F32)
        for j, g in enumerate(gs):
            _store_packed_rows(ybuf, row0[j], y[j * EXPERT_BLOCK:(j + 1) * EXPERT_BLOCK])
            out_copy(g).start()

            @pl.when(g + EXPERT_RING < n_used)
            def _():
                in_copy(g + EXPERT_RING).start()

    first = first_ref[e]
    n = nblk_ref[e]

    def pair(p, c):
        consume(first + 2 * p, 2)
        return c

    lax.fori_loop(0, n // 2, pair, 0)

    @pl.when(n % 2 == 1)
    def _():
        consume(first + n - 1, 1)

    @pl.when(e == last)
    def _():
        for g in range(EXPERT_RING):
            out_copy(g).wait()
        ybuf[0:blk_rows, :] = jnp.zeros((blk_rows, ybuf.shape[1]), ybuf.dtype)

        def start(b, c):
            pltpu.make_async_copy(ybuf.at[rows_of(0)], ys_hbm.at[rows_of(b)], osem.at[0]).start()
            return c

        def wait(b, c):
            pltpu.make_async_copy(ybuf.at[rows_of(0)], ys_hbm.at[rows_of(b)], osem.at[0]).wait()
            return c

        lax.fori_loop(tail_ref[0], tail_ref[1], start, 0)
        lax.fori_loop(tail_ref[0], tail_ref[1], wait, 0)


def _expert_call(first_blk, n_blk, tail, xs, w_gate_e, w_up_e, w_down_e):
    n_exp, d, hid = w_gate_e.shape
    blk_rows = EXPERT_BLOCK * (d // 2 // LANES)
    grid_spec = pltpu.PrefetchScalarGridSpec(
        num_scalar_prefetch=3,
        grid=(n_exp,),
        in_specs=[pl.BlockSpec(memory_space=pl.ANY),
                  pl.BlockSpec((1, d, hid), lambda e, *_: (e, 0, 0)),
                  pl.BlockSpec((1, d, hid), lambda e, *_: (e, 0, 0)),
                  pl.BlockSpec((1, hid, d), lambda e, *_: (e, 0, 0))],
        out_specs=pl.BlockSpec(memory_space=pl.ANY),
        scratch_shapes=[pltpu.VMEM((d, 2 * hid), BF16), pltpu.VMEM((hid, d), BF16),
                        pltpu.VMEM((EXPERT_RING * blk_rows, LANES), U32),
                        pltpu.VMEM((EXPERT_RING * blk_rows, LANES), U32),
                        pltpu.SemaphoreType.DMA((EXPERT_RING,)),
                        pltpu.SemaphoreType.DMA((EXPERT_RING,))],
    )
    return pl.pallas_call(
        _expert_kernel,
        grid_spec=grid_spec,
        out_shape=jax.ShapeDtypeStruct(xs.shape, U32),
        compiler_params=pltpu.CompilerParams(
            dimension_semantics=("arbitrary",), vmem_limit_bytes=VMEM_LIMIT_OTHER),
        name="experts",
    )(first_blk, n_blk, tail, xs, w_gate_e, w_up_e, w_down_e)


def _combine_kernel(tab_ref, next_tab_ref, x1_ref, bpos_ref, w_ref, ys_hbm, wsgu_ref, wsd_ref,
                    g2_ref, b2_ref, out_ref, buf, acc, bp_s, wq_s, sem, *, alpha):
    tt = x1_ref.shape[0]
    hid = wsd_ref.shape[0]
    sub = buf.shape[0] // BUF_ROWS
    chunk_rows = RUN_ROWS * sub
    n_exp = tab_ref.shape[2]
    cpb = SORT_BLOCK // RUN_ROWS
    base_blocks = n_exp // cpb
    i = pl.program_id(0)

    def make_copy(chunk, slot):
        return pltpu.make_async_copy(
            ys_hbm.at[pl.ds(pl.multiple_of(slot * sub, 2 * sub), chunk_rows)],
            buf.at[pl.ds(pl.multiple_of(chunk * chunk_rows, chunk_rows), chunk_rows)],
            sem.at[chunk // cpb])

    def base_block_copies(kb):
        return pltpu.make_async_copy(
            ys_hbm.at[pl.ds(0, cpb * chunk_rows)],
            buf.at[pl.ds(kb * cpb * chunk_rows, cpb * chunk_rows)], sem.at[kb])

    @pl.when(i == 0)
    def _():
        buf[...] = jnp.zeros_like(buf)
        _start_chunks(tab_ref, n_exp, 1, make_copy)

    @pl.when((i > 0) & (tab_ref[0, TAB_META, 1] > 1))
    def _():
        for kb in range(base_blocks):
            base_block_copies(kb).wait()
        _start_chunks(tab_ref, n_exp, 1, make_copy)

    x1 = x1_ref[...]
    gu = jnp.dot(x1.astype(BF16), wsgu_ref[...], preferred_element_type=F32)
    hs = _silu(gu[:, :hid]) * gu[:, hid:]
    acc[...] = jnp.dot(hs.astype(BF16), wsd_ref[...], preferred_element_type=F32)

    for k in range(TOP_K):
        bp_s[k] = jnp.broadcast_to(bpos_ref[:, k:k + 1], (tt, LANES)).astype(I16)
        wq_s[k] = jnp.broadcast_to(w_ref[:, k:k + 1], (tt, LANES)).astype(BF16)

    def unsort_block(kb):
        qs = []
        for c0 in range(kb * SORT_BLOCK, (kb + 1) * SORT_BLOCK, LANES):
            iol = (lax.broadcasted_iota(I32, (tt, LANES), 1) + c0).astype(I16)
            q = jnp.zeros((tt, LANES), BF16)
            for k in range(TOP_K):
                q = jnp.where(iol == bp_s[k], wq_s[k], q)
            qs.append(q)
        rows = _load_packed_rows(buf, kb * SORT_BLOCK, SORT_BLOCK, sub)
        acc[...] += jnp.dot(jnp.concatenate(qs, axis=1), rows, preferred_element_type=F32)

    for kb in range(base_blocks):
        base_block_copies(kb).wait()
        unsort_block(kb)
        for u in range(cpb):
            e = kb * cpb + u
            make_copy(e, next_tab_ref[0, TAB_SLOT, e]).start(priority=u % 2)

    n_ch = tab_ref[0, TAB_META, 0]
    for kb in range(base_blocks, BUF_ROWS // SORT_BLOCK):
        @pl.when(kb * cpb < n_ch)
        def _():
            def wait(j, c):
                make_copy(kb * cpb, 0).wait()
                return c
            lax.fori_loop(0, jnp.minimum(n_ch - kb * cpb, cpb), wait, 0)
            unsort_block(kb)

    out_ref[...] = _layernorm(alpha * x1 + acc[...], g2_ref[...], b2_ref[...])

    @pl.when(i == pl.num_programs(0) - 1)
    def _():
        for kb in range(base_blocks):
            base_block_copies(kb).wait()


def _combine_call(tab, x1, bpos_tok, w_tok, ys, ws_gu, ws_d, ln2_g, ln2_b, *, alpha):
    t, d = x1.shape
    tt = TILE
    n_tiles = t // tt
    n_exp = tab.shape[2]

    def const(shape):
        return pl.BlockSpec(shape, lambda i: (0,) * len(shape))

    tab_spec = lambda f: pl.BlockSpec((1, 4, n_exp), f, memory_space=pltpu.SMEM)
    return pl.pallas_call(
        functools.partial(_combine_kernel, alpha=alpha),
        grid=(n_tiles,),
        in_specs=[tab_spec(lambda i: (i, 0, 0)),
                  tab_spec(lambda i: (jnp.minimum(i + 1, n_tiles - 1), 0, 0)),
                  pl.BlockSpec((tt, d), lambda i: (i, 0)),
                  pl.BlockSpec((tt, TOP_K), lambda i: (i, 0)),
                  pl.BlockSpec((tt, TOP_K), lambda i: (i, 0)),
                  pl.BlockSpec(memory_space=pl.ANY),
                  const(ws_gu.shape), const(ws_d.shape), const(ln2_g.shape), const(ln2_b.shape)],
        out_specs=pl.BlockSpec((tt, d), lambda i: (i, 0)),
        out_shape=jax.ShapeDtypeStruct((t, d), F32),
        scratch_shapes=[pltpu.VMEM((BUF_ROWS * (d // 2 // LANES), LANES), U32),
                        pltpu.VMEM((tt, d), F32),
                        pltpu.VMEM((TOP_K, tt, LANES), I16),
                        pltpu.VMEM((TOP_K, tt, LANES), BF16),
                        pltpu.SemaphoreType.DMA((BUF_ROWS // SORT_BLOCK,))],
        compiler_params=pltpu.CompilerParams(
            dimension_semantics=("arbitrary",), vmem_limit_bytes=VMEM_LIMIT_OTHER),
        name="combine",
    )(tab, tab, x1, bpos_tok, w_tok, ys, ws_gu, ws_d, ln2_g, ln2_b)


def _layer(x, w_in, pool_w, pool_scale, conv_dw, conv_ln_g, conv_ln_b, conv_w_out, w_out,
           ln1_g, ln1_b, w_router, router_bias, w_gate_e, w_up_e, w_down_e,
           ws_gate, ws_up, ws_down, ln2_g, ln2_b, *, alpha):
    b, s, d = x.shape
    t = b * s
    row = lambda v: v.reshape(1, -1)
    x1, logits_t = _mixer_call(
        x, w_in.astype(BF16), pool_w.astype(BF16), row(pool_scale), conv_dw, row(conv_ln_g),
        row(conv_ln_b), conv_w_out.astype(BF16), w_out.astype(BF16), row(ln1_g), row(ln1_b),
        w_router.T.astype(BF16), alpha=alpha)
    w_t, lpos_t, bpos_t, rtab, counts = _route_call(logits_t, router_bias.reshape(-1, 1))

    n = t * TOP_K
    n_blocks = -(-(n + (t // TILE) * N_EXPERTS + N_EXPERTS * (EXPERT_BLOCK - 1 + RUN_ROWS))
                 // EXPERT_BLOCK)
    n_slots = n_blocks * EXPERT_BLOCK
    counts = counts[0].astype(I32)
    n_blk = (counts + RUN_ROWS + EXPERT_BLOCK - 1) // EXPERT_BLOCK
    end_blk = jnp.cumsum(n_blk)
    first_blk = end_blk - n_blk
    pad_start = first_blk * EXPERT_BLOCK
    n_used = end_blk[-1]
    blk = jnp.arange(n_blocks, dtype=I32)
    block_e = jnp.minimum(jnp.sum((end_blk[None, :] <= blk[:, None]).astype(I32), axis=1),
                          N_EXPERTS - 1)
    real_end = (pad_start + counts)[block_e]
    zflag = ((blk >= n_used) | ((blk + 1) * EXPERT_BLOCK > real_end)).astype(I32)
    tail = jnp.stack([n_used, jnp.asarray(n_blocks, I32)])

    rtab = rtab.astype(I32)
    nch = rtab[:, ROUTE_NCH, :]
    meta = jnp.zeros_like(nch).at[:, 0].set(jnp.sum(nch, axis=1)).at[:, 1].set(jnp.max(nch, axis=1))
    slot = rtab[:, ROUTE_BASE, :] + pad_start[None, :]
    length = rtab[:, ROUTE_LEN, :]
    halves = jnp.where(length <= RUN_ROWS // 2, 1, 2) + 2 * (nch - 1)
    meta_d = meta.at[:, 0].set(jnp.sum(halves, axis=1))
    tab_d = jnp.stack([slot, nch, rtab[:, ROUTE_START, :], meta_d, length], axis=1)
    tab_c = jnp.stack([slot, nch, rtab[:, ROUTE_CHUNK, :], meta], axis=1)

    xs = _dispatch_call(zflag, tab_d, lpos_t, x1, n_slots)
    ys = _expert_call(first_blk, n_blk, tail, xs, w_gate_e, w_up_e, w_down_e)
    ws_gu = jnp.concatenate([ws_gate, ws_up], axis=1).astype(BF16)
    out = _combine_call(tab_c, x1, bpos_t.T, w_t.T, ys, ws_gu, ws_down.astype(BF16), row(ln2_g),
                        row(ln2_b), alpha=alpha)
    return out.reshape(b, s, d)


def kernel(x, w_in, pool_w, pool_scale, conv_dw, conv_ln_g, conv_ln_b, conv_w_out, w_out, ln1_g, ln1_b, w_router, router_bias, w_gate_e, w_up_e, w_down_e, ws_gate, ws_up, ws_down, ln2_g, ln2_b):
    depth = w_in.shape[0]
    alpha = (2.0 * depth) ** 0.25
    for l in range(depth):
        x = _layer(x, w_in[l], pool_w[l], pool_scale[l], conv_dw[l], conv_ln_g[l], conv_ln_b[l],
                   conv_w_out[l], w_out[l], ln1_g[l], ln1_b[l], w_router[l], router_bias[l],
                   w_gate_e[l], w_up_e[l], w_down_e[l], ws_gate[l], ws_up[l], ws_down[l],
                   ln2_g[l], ln2_b[l], alpha=alpha)
    return x
```
